```python
import math, functools
import jax, jax.numpy as jnp
from jax import lax
import numpy as np

D_MODEL = 4096
BATCH = 4
SEQ = 2048
DEPTH = 4
DEC_BATCH = 128
DEC_SEQ = 4
PAST_LEN = 16384
PAGE_SIZE = 128

N_MIXERS = 3
N_META = 16
CHUNK = 128
EPS = 1e-6

SSD_EXPAND = 2
SSD_D_INNER = SSD_EXPAND * D_MODEL
SSD_HEAD_DIM = 64
SSD_HEADS = SSD_D_INNER // SSD_HEAD_DIM
SSD_STATE = 128
SSD_GROUPS = 8
SSD_CONV_W = 4
SSD_CONV_DIM = SSD_D_INNER + 2 * SSD_GROUPS * SSD_STATE
SSD_IN = SSD_D_INNER + SSD_CONV_DIM + SSD_HEADS

ML_HEADS = 8
ML_QK = D_MODEL // 2
ML_V = D_MODEL
ML_QK_HEAD = ML_QK // ML_HEADS
ML_V_HEAD = ML_V // ML_HEADS
ML_IN = 2 * ML_QK + 3 * ML_V + 2 * ML_HEADS

RET_HEADS = 16
RET_QK = D_MODEL
RET_V = 2 * D_MODEL
RET_QK_HEAD = RET_QK // RET_HEADS
RET_V_HEAD = RET_V // RET_HEADS
RET_IN = 2 * RET_QK + 2 * RET_V
ROPE_BASE = 10000.0

N_SSD = (DEPTH + 2) // 3
N_ML = (DEPTH + 1) // 3
N_RET = DEPTH // 3

kernel_name = 'hybrid_ssd_mlstm_retention_step'


def _rmsnorm(x, g):
    x32 = x.astype(jnp.float32)
    return x32 * lax.rsqrt(jnp.mean(x32 * x32, axis=-1, keepdims=True) + EPS) * g.astype(jnp.float32)


def _group_rmsnorm(x, g, n_groups):
    xs = x.reshape(*x.shape[:-1], n_groups, x.shape[-1] // n_groups)
    xs = xs * lax.rsqrt(jnp.mean(xs * xs, axis=-1, keepdims=True) + EPS)
    return xs.reshape(x.shape) * g.astype(jnp.float32)


def _chunk_len(L):
    return CHUNK if L % CHUNK == 0 else L


def _to_chunks(a, lc):
    return jnp.moveaxis(a.reshape(a.shape[0], a.shape[1] // lc, lc, *a.shape[2:]), 1, 0)


def _from_chunks(a):
    a = jnp.moveaxis(a, 0, 1)
    return a.reshape(a.shape[0], a.shape[1] * a.shape[2], *a.shape[3:])


def _chunked_scan(step, seqs, state):
    lc = _chunk_len(seqs[0].shape[1])
    state, ys = lax.scan(step, state, tuple(_to_chunks(a, lc) for a in seqs))
    return _from_chunks(ys), state


def _segmented(step, seqs, state, n_lead):
    if n_lead == 0:
        return _chunked_scan(step, seqs, state)
    y_lead, state = _chunked_scan(step, tuple(a[:, :n_lead] for a in seqs), state)
    y_rest, state = _chunked_scan(step, tuple(a[:, n_lead:] for a in seqs), state)
    return jnp.concatenate([y_lead, y_rest], axis=1), state


def _causal(l):
    return jnp.tril(jnp.ones((l, l), dtype=bool))


def _ssd_step(A, h, xs):
    x, dt, Bm, Cm = xs
    b, l = x.shape[:2]
    hg = SSD_HEADS // SSD_GROUPS
    cum = jnp.cumsum(dt * A, axis=1)
    seg = cum[:, :, None, :] - cum[:, None, :, :]
    decay = jnp.exp(jnp.where(_causal(l)[None, :, :, None], seg, -jnp.inf))
    cb = jnp.einsum('btgn,bsgn->btsg', Cm, Bm)
    w = (decay * dt[:, None, :, :]).reshape(b, l, l, SSD_GROUPS, hg) * cb[..., None]
    xg = x.reshape(b, l, SSD_GROUPS, hg, SSD_HEAD_DIM)
    hgp = h.reshape(b, SSD_GROUPS, hg, SSD_HEAD_DIM, SSD_STATE)
    y = (jnp.einsum('btsgj,bsgjp->btgjp', w, xg)
         + jnp.einsum('btgn,bgjpn->btgjp', Cm, hgp) * jnp.exp(cum).reshape(b, l, SSD_GROUPS, hg, 1))
    to_end = (jnp.exp(cum[:, -1:, :] - cum) * dt).reshape(b, l, SSD_GROUPS, hg)
    h_new = (hgp * jnp.exp(cum[:, -1]).reshape(b, SSD_GROUPS, hg, 1, 1)
             + jnp.einsum('bsgj,bsgjp,bsgn->bgjpn', to_end, xg, Bm))
    return h_new.reshape(h.shape), y.reshape(x.shape)


def _ssd_mixer(h, conv_buf, ssm, w_in, conv_w, conv_b, dt_bias, A_log, d_skip, norm_g, w_out, n_lead):
    f32 = jnp.float32
    b, L = h.shape[:2]
    proj = h @ w_in
    z, xbc, dt = jnp.split(proj, [SSD_D_INNER, SSD_D_INNER + SSD_CONV_DIM], axis=-1)
    padded = jnp.concatenate([conv_buf.astype(xbc.dtype), xbc], axis=1)
    new_buf = padded[:, -(SSD_CONV_W - 1):]
    acc = conv_b.astype(f32)
    for k in range(SSD_CONV_W):
        acc = acc + padded[:, k:k + L].astype(f32) * conv_w[k].astype(f32)
    xbc = jax.nn.silu(acc)
    x, Bm, Cm = jnp.split(xbc, [SSD_D_INNER, SSD_D_INNER + SSD_GROUPS * SSD_STATE], axis=-1)
    x = x.reshape(b, L, SSD_HEADS, SSD_HEAD_DIM)
    Bm = Bm.reshape(b, L, SSD_GROUPS, SSD_STATE)
    Cm = Cm.reshape(b, L, SSD_GROUPS, SSD_STATE)
    dt = jax.nn.softplus(dt.astype(f32) + dt_bias.astype(f32))
    A = -jnp.exp(A_log.astype(f32))
    y, ssm_new = _segmented(functools.partial(_ssd_step, A), (x, dt, Bm, Cm), ssm.astype(f32), n_lead)
    y = y + x * d_skip.astype(f32)[:, None]
    y = y.reshape(b, L, SSD_D_INNER) * jax.nn.silu(z.astype(f32))
    y = _group_rmsnorm(y, norm_g, SSD_GROUPS)
    return y.astype(h.dtype) @ w_out, new_buf, ssm_new


def _mlstm_step(carry, xs):
    C, n, m = carry
    q, k, v, ig, lf = xs
    l = q.shape[1]
    bcum = jnp.cumsum(lf, axis=1)
    logd = jnp.where(_causal(l)[None, :, :, None],
                     bcum[:, :, None, :] - bcum[:, None, :, :] + ig[:, None, :, :], -jnp.inf)
    m_inter = bcum + m[:, None, :]
    m_t = jnp.maximum(m_inter, jnp.max(logd, axis=2))
    dmat = jnp.exp(logd - m_t[:, :, None, :])
    w_prev = jnp.exp(m_inter - m_t)
    s = jnp.einsum('bthd,bshd->btsh', q, k) * dmat
    num = jnp.einsum('btsh,bshv->bthv', s, v) + w_prev[..., None] * jnp.einsum('bthd,bhdv->bthv', q, C)
    den = jnp.sum(s, axis=2) + w_prev * jnp.einsum('bthd,bhd->bth', q, n)
    hout = num / jnp.maximum(jnp.abs(den), jnp.exp(-m_t))[..., None]
    m_new = m_t[:, -1]
    w_end = jnp.exp(bcum[:, -1:, :] - bcum + ig - m_new[:, None, :])
    dec = jnp.exp(bcum[:, -1] + m - m_new)
    C_new = dec[..., None, None] * C + jnp.einsum('bsh,bshd,bshv->bhdv', w_end, k, v)
    n_new = dec[..., None] * n + jnp.einsum('bsh,bshd->bhd', w_end, k)
    return (C_new, n_new, m_new), hout


def _mlstm_mixer(h, C, n, m, w_in, b_gates, norm_g, w_out, n_lead):
    f32 = jnp.float32
    b, L = h.shape[:2]
    proj = h @ w_in
    q, k, v, o, z, gates = jnp.split(
        proj, [ML_QK, 2 * ML_QK, 2 * ML_QK + ML_V, 2 * ML_QK + 2 * ML_V, 2 * ML_QK + 3 * ML_V], axis=-1)
    q = q.astype(f32).reshape(b, L, ML_HEADS, ML_QK_HEAD)
    k = k.astype(f32).reshape(b, L, ML_HEADS, ML_QK_HEAD) * (ML_QK_HEAD ** -0.5)
    v = v.astype(f32).reshape(b, L, ML_HEADS, ML_V_HEAD)
    gates = gates.astype(f32) + b_gates.astype(f32)
    ig = gates[..., :ML_HEADS]
    lf = jax.nn.log_sigmoid(gates[..., ML_HEADS:])
    hc, (C, n, m) = _segmented(_mlstm_step, (q, k, v, ig, lf),
                               (C.astype(f32), n.astype(f32), m.astype(f32)), n_lead)
    hc = hc.reshape(b, L, ML_V) * jax.nn.sigmoid(o.astype(f32))
    hc = _group_rmsnorm(hc, norm_g, ML_HEADS) * jax.nn.silu(z.astype(f32))
    return hc.astype(h.dtype) @ w_out, C, n, m


def _rope(x, pos):
    half = x.shape[-1] // 2
    inv = ROPE_BASE ** (-jnp.arange(half, dtype=jnp.float32) / half)
    ang = pos.astype(jnp.float32)[:, None] * inv
    cos, sin = jnp.cos(ang)[None, :, None, :], jnp.sin(ang)[None, :, None, :]
    x1, x2 = x[..., :half], x[..., half:]
    return jnp.concatenate([x1 * cos - x2 * sin, x1 * sin + x2 * cos], axis=-1)


def _ret_step(lg, S, xs):
    q, k, v = xs
    l = q.shape[1]
    idx = jnp.arange(l, dtype=jnp.float32)
    rel = idx[:, None] - idx[None, :]
    dmat = jnp.exp(jnp.where((rel >= 0)[:, :, None], rel[:, :, None] * lg, -jnp.inf))
    a = jnp.einsum('bthd,bshd->btsh', q, k) * dmat[None]
    y = (jnp.einsum('btsh,bshv->bthv', a, v)
         + jnp.einsum('bthd,bhdv->bthv', q, S) * jnp.exp((idx + 1.0)[:, None] * lg)[None, :, :, None])
    w_end = jnp.exp((l - 1.0 - idx)[:, None] * lg)
    S_new = jnp.exp(l * lg)[None, :, None, None] * S + jnp.einsum('sh,bshd,bshv->bhdv', w_end, k, v)
    return S_new, y


def _ret_mixer(h, S, pos, w_in, norm_g, w_out, n_lead):
    f32 = jnp.float32
    b, L = h.shape[:2]
    proj = h @ w_in
    q, k, v, g = jnp.split(proj, [RET_QK, 2 * RET_QK, 2 * RET_QK + RET_V], axis=-1)
    q = _rope(q.astype(f32).reshape(b, L, RET_HEADS, RET_QK_HEAD), pos)
    k = _rope(k.astype(f32).reshape(b, L, RET_HEADS, RET_QK_HEAD), pos) * (RET_QK_HEAD ** -0.5)
    v = v.astype(f32).reshape(b, L, RET_HEADS, RET_V_HEAD)
    lg = jnp.log1p(-jnp.exp2(-5.0 - jnp.arange(RET_HEADS, dtype=f32)))
    y, S_new = _segmented(functools.partial(_ret_step, lg), (q, k, v), S.astype(f32), n_lead)
    y = _group_rmsnorm(y.reshape(b, L, RET_V), norm_g, RET_HEADS) * jax.nn.silu(g.astype(f32))
    return y.astype(h.dtype) @ w_out, S_new


def _forward(x, ssm, conv, mC, mn, mm, ret, pos, n_lead, w):
    (pre_g, post_g, ssd_w_in, ssd_conv_w, ssd_conv_b, ssd_dt_bias, ssd_A_log, ssd_D, ssd_norm_g,
     ssd_w_out, ml_w_in, ml_b_gates, ml_norm_g, ml_w_out, ret_w_in, ret_norm_g, ret_w_out) = w
    n_ssm, n_conv, n_C, n_n, n_m, n_ret = [], [], [], [], [], []
    for i in range(DEPTH):
        kind, j = i % N_MIXERS, i // N_MIXERS
        h = _rmsnorm(x, pre_g[i]).astype(x.dtype)
        if kind == 0:
            out, cb, st = _ssd_mixer(h, conv[j], ssm[j], ssd_w_in[j], ssd_conv_w[j], ssd_conv_b[j],
                                     ssd_dt_bias[j], ssd_A_log[j], ssd_D[j], ssd_norm_g[j], ssd_w_out[j], n_lead)
            n_conv.append(cb)
            n_ssm.append(st)
        elif kind == 1:
            out, c_, nn_, m_ = _mlstm_mixer(h, mC[j], mn[j], mm[j], ml_w_in[j], ml_b_gates[j],
                                           ml_norm_g[j], ml_w_out[j], n_lead)
            n_C.append(c_)
            n_n.append(nn_)
            n_m.append(m_)
        else:
            out, s_ = _ret_mixer(h, ret[j], pos, ret_w_in[j], ret_norm_g[j], ret_w_out[j], n_lead)
            n_ret.append(s_)
        x = x + _rmsnorm(out, post_g[i]).astype(x.dtype)
    return (x, jnp.stack(n_ssm), jnp.stack(n_conv), jnp.stack(n_C), jnp.stack(n_n),
            jnp.stack(n_m), jnp.stack(n_ret))


def setup_inputs(seed: int = 0) -> dict:
    key = jax.random.key(seed)
    ks = jax.random.split(key, 28)
    f32 = jnp.float32

    def nrm(k, shape, scale):
        return scale * jax.random.normal(k, shape, f32)

    dt0 = jnp.exp(jax.random.uniform(ks[13], (N_SSD, SSD_HEADS), f32, math.log(1e-3), math.log(1e-1)))
    f_bias = jnp.linspace(3.0, 6.0, ML_HEADS, dtype=f32)
    ml_b_gates = jnp.concatenate(
        [nrm(ks[20], (N_ML, ML_HEADS), 0.1), f_bias[None] + nrm(ks[21], (N_ML, ML_HEADS), 0.1)], axis=-1)
    return {
        'x_prompt': nrm(ks[0], (BATCH, SEQ, D_MODEL), 1.0),
        'x_sample': nrm(ks[1], (DEC_BATCH, DEC_SEQ, D_MODEL), 1.0),
        'state_ssm': nrm(ks[2], (N_SSD, DEC_BATCH, SSD_HEADS, SSD_HEAD_DIM, SSD_STATE), 0.3),
        'state_ssd_conv': nrm(ks[3], (N_SSD, DEC_BATCH, SSD_CONV_W - 1, SSD_CONV_DIM), 1.0),
        'state_mlstm_C': nrm(ks[4], (N_ML, DEC_BATCH, ML_HEADS, ML_QK_HEAD, ML_V_HEAD), 0.3),
        'state_mlstm_n': nrm(ks[5], (N_ML, DEC_BATCH, ML_HEADS, ML_QK_HEAD), 0.3),
        'state_mlstm_m': nrm(ks[6], (N_ML, DEC_BATCH, ML_HEADS), 1.0),
        'state_ret': nrm(ks[7], (N_RET, DEC_BATCH, RET_HEADS, RET_QK_HEAD, RET_V_HEAD), 0.1),
        'meta_tokens': nrm(ks[8], (N_META, D_MODEL), 1.0),
        'pre_norm_g': 1.0 + nrm(ks[9], (DEPTH, D_MODEL), 0.05),
        'post_norm_g': 1.0 + nrm(ks[10], (DEPTH, D_MODEL), 0.05),
        'ssd_w_in': nrm(ks[11], (N_SSD, D_MODEL, SSD_IN), D_MODEL ** -0.5),
        'ssd_conv_w': nrm(ks[12], (N_SSD, SSD_CONV_W, SSD_CONV_DIM), 0.5),
        'ssd_conv_b': nrm(ks[14], (N_SSD, SSD_CONV_DIM), 0.1),
        'ssd_dt_bias': dt0 + jnp.log(-jnp.expm1(-dt0)),
        'ssd_A_log': jnp.log(jax.random.uniform(ks[15], (N_SSD, SSD_HEADS), f32, 1.0, 16.0)),
        'ssd_D': 1.0 + nrm(ks[16], (N_SSD, SSD_HEADS), 0.1),
        'ssd_norm_g': 1.0 + nrm(ks[17], (N_SSD, SSD_D_INNER), 0.05),
        'ssd_w_out': nrm(ks[18], (N_SSD, SSD_D_INNER, D_MODEL), SSD_D_INNER ** -0.5),
        'ml_w_in': nrm(ks[19], (N_ML, D_MODEL, ML_IN), D_MODEL ** -0.5),
        'ml_b_gates': ml_b_gates,
        'ml_norm_g': 1.0 + nrm(ks[22], (N_ML, ML_V), 0.05),
        'ml_w_out': nrm(ks[23], (N_ML, ML_V, D_MODEL), ML_V ** -0.5),
        'ret_w_in': nrm(ks[24], (N_RET, D_MODEL, RET_IN), D_MODEL ** -0.5),
        'ret_norm_g': 1.0 + nrm(ks[25], (N_RET, RET_V), 0.05),
        'ret_w_out': nrm(ks[26], (N_RET, RET_V, D_MODEL), RET_V ** -0.5),
    }


def reference(x_prompt, x_sample, state_ssm, state_ssd_conv, state_mlstm_C, state_mlstm_n, state_mlstm_m,
              state_ret, meta_tokens, pre_norm_g, post_norm_g, ssd_w_in, ssd_conv_w, ssd_conv_b, ssd_dt_bias,
              ssd_A_log, ssd_D, ssd_norm_g, ssd_w_out, ml_w_in, ml_b_gates, ml_norm_g, ml_w_out,
              ret_w_in, ret_norm_g, ret_w_out):
    f32 = jnp.float32
    w = (pre_norm_g, post_norm_g, ssd_w_in, ssd_conv_w, ssd_conv_b, ssd_dt_bias, ssd_A_log, ssd_D,
         ssd_norm_g, ssd_w_out, ml_w_in, ml_b_gates, ml_norm_g, ml_w_out, ret_w_in, ret_norm_g, ret_w_out)

    bp, sp = x_prompt.shape[:2]
    meta = jnp.broadcast_to(meta_tokens.astype(x_prompt.dtype)[None], (bp, N_META, D_MODEL))
    xp = jnp.concatenate([meta, x_prompt], axis=1)
    yp, ssm_p, conv_p, mC_p, mn_p, mm_p, ret_p = _forward(
        xp,
        jnp.zeros((N_SSD, bp, SSD_HEADS, SSD_HEAD_DIM, SSD_STATE), f32),
        jnp.zeros((N_SSD, bp, SSD_CONV_W - 1, SSD_CONV_DIM), x_prompt.dtype),
        jnp.zeros((N_ML, bp, ML_HEADS, ML_QK_HEAD, ML_V_HEAD), f32),
        jnp.zeros((N_ML, bp, ML_HEADS, ML_QK_HEAD), f32),
        jnp.zeros((N_ML, bp, ML_HEADS), f32),
        jnp.zeros((N_RET, bp, RET_HEADS, RET_QK_HEAD, RET_V_HEAD), f32),
        jnp.arange(N_META + sp), N_META, w)
    y_prompt = yp[:, N_META:]

    y_sample, ssm_s, conv_s, mC_s, mn_s, mm_s, ret_s = _forward(
        x_sample, state_ssm, state_ssd_conv, state_mlstm_C, state_mlstm_n, state_mlstm_m, state_ret,
        PAST_LEN + jnp.arange(x_sample.shape[1]), 0, w)

    return (y_prompt, y_sample, ssm_p, conv_p, mC_p, mn_p, mm_p, ret_p,
            ssm_s, conv_s, mC_s, mn_s, mm_s, ret_s)
```

```python
import functools
import math

import jax
import jax.numpy as jnp
from jax import lax
from jax.experimental import pallas as pl
from jax.experimental.pallas import tpu as pltpu

F32 = jnp.float32
BF16 = jnp.bfloat16

EPS = 1e-6
CHUNK = 128
LANES = 128
PAST_LEN = 16384
ROPE_BASE = 10000.0
K_BLOCK = 4096
NEG_BIG = -1e30
VMEM_LIMIT = 56 * 1024 * 1024

SSD_GROUPS = 8
SSD_HEAD_DIM = 64
SSD_CONV_W = 4
ML_HEADS = 8
RET_HEADS = 16


def _dot(a, b):
    return jnp.dot(a, b, preferred_element_type=F32)


def _dot_nt(a, b):
    return lax.dot_general(a, b, (((1,), (1,)), ((), ())), preferred_element_type=F32)


def _dot_tn(a, b):
    return lax.dot_general(a, b, (((0,), (0,)), ((), ())), preferred_element_type=F32)


def _split3(x):
    hi = x.astype(BF16)
    r1 = x - hi.astype(F32)
    mid = r1.astype(BF16)
    lo = (r1 - mid.astype(F32)).astype(BF16)
    return hi, mid, lo


def _cum_left(tri, x):
    hi, mid, lo = _split3(x)
    return _dot(tri, hi) + _dot(tri, mid) + _dot(tri, lo)


def _cum_right(x, tri):
    hi, mid, lo = _split3(x)
    return _dot(hi, tri) + _dot(mid, tri) + _dot(lo, tri)


def _tri(n, lower):
    r = lax.broadcasted_iota(jnp.int32, (n, n), 0)
    c = lax.broadcasted_iota(jnp.int32, (n, n), 1)
    m = (r >= c) if lower else (r <= c)
    return jnp.where(m, 1.0, 0.0).astype(BF16)


def _softplus(x):
    return jnp.maximum(x, 0.0) + jnp.log1p(jnp.exp(-jnp.abs(x)))


def _log_sigmoid(x):
    return jnp.minimum(x, 0.0) - jnp.log1p(jnp.exp(-jnp.abs(x)))


def _silu(x):
    return x * jax.nn.sigmoid(x)


def _pad_rows(val, pad_ref, fill=0.0):
    rb = val.shape[0]
    if rb == CHUNK:
        return val
    pad_ref[...] = jnp.full(pad_ref.shape, fill, pad_ref.dtype)
    pad_ref[0:rb, :] = val
    return pad_ref[...]


def _pad_lanes(val, pad_ref, fill=0.0):
    rb = val.shape[1]
    if rb == CHUNK:
        return val
    pad_ref[...] = jnp.full(pad_ref.shape, fill, pad_ref.dtype)
    pad_ref[:, 0:rb] = val
    return pad_ref[...]


def _row_tile(n, cap):
    best = None
    for t in range(16, min(n, cap) + 1, 16):
        if n % t == 0:
            best = t
    return best if best is not None else n


def _params(sem, limit=VMEM_LIMIT):
    return pltpu.CompilerParams(dimension_semantics=sem, vmem_limit_bytes=limit)


def _norm_body(*refs, has_post, has_pre, has_side):
    refs = list(refs)
    if has_post:
        o_ref, x_ref, gpost_ref = refs[:3]
        refs = refs[3:]
    else:
        x_ref = refs[0]
        refs = refs[1:]
    if has_pre:
        gpre_ref = refs[0]
        refs = refs[1:]
    if has_side:
        ws_ref = refs[0]
        refs = refs[1:]
    outs = refs
    x = x_ref[...]
    if has_post:
        o = o_ref[...]
        x = x + o * lax.rsqrt(jnp.mean(o * o, axis=-1, keepdims=True) + EPS) * gpost_ref[...]
        outs[0][...] = x
        outs = outs[1:]
    if has_pre:
        h = (x * lax.rsqrt(jnp.mean(x * x, axis=-1, keepdims=True) + EPS) * gpre_ref[...]).astype(BF16)
        outs[0][...] = h
        if has_side:
            outs[1][...] = _dot(h, ws_ref[...].astype(BF16))


def _norm_call(x, o=None, g_post=None, g_pre=None, w_side=None):
    R, D = x.shape
    has_post, has_pre, has_side = o is not None, g_pre is not None, w_side is not None
    tr = _row_tile(R, 256)
    row = pl.BlockSpec((tr, D), lambda i: (i, 0))
    vec = pl.BlockSpec((1, D), lambda i: (0, 0))
    ins, specs = [], []
    if has_post:
        ins += [o, x, g_post.reshape(1, D)]
        specs += [row, row, vec]
    else:
        ins += [x]
        specs += [row]
    if has_pre:
        ins += [g_pre.reshape(1, D)]
        specs += [vec]
    if has_side:
        ins += [w_side]
        specs += [pl.BlockSpec(w_side.shape, lambda i: (0, 0))]
    out_shape, out_specs = [], []
    if has_post:
        out_shape.append(jax.ShapeDtypeStruct((R, D), F32))
        out_specs.append(row)
    if has_pre:
        out_shape.append(jax.ShapeDtypeStruct((R, D), BF16))
        out_specs.append(row)
        if has_side:
            out_shape.append(jax.ShapeDtypeStruct((R, w_side.shape[1]), F32))
            out_specs.append(pl.BlockSpec((tr, w_side.shape[1]), lambda i: (i, 0)))
    return pl.pallas_call(
        functools.partial(_norm_body, has_post=has_post, has_pre=has_pre, has_side=has_side),
        grid=(R // tr,), in_specs=specs, out_specs=out_specs, out_shape=out_shape,
        compiler_params=_params(("parallel",)), name="norm")(*ins)


def _mm_body(x_ref, w_ref, o_ref):
    o_ref[...] = _dot(x_ref[...], w_ref[...].astype(BF16))


def _mm_acc_body(x_ref, w_ref, a_ref, o_ref):
    o_ref[...] = a_ref[...] + _dot(x_ref[...], w_ref[...].astype(BF16))


def _matmul(x, w, n_cols):
    R, K = x.shape
    tm = _row_tile(R, 2048)
    tn = 256
    out = None
    for kb in range(K // K_BLOCK):
        x_spec = pl.BlockSpec((tm, K_BLOCK), lambda i, j, kb=kb: (i, kb))
        w_spec = pl.BlockSpec((K_BLOCK, tn), lambda i, j, kb=kb: (kb, j))
        o_spec = pl.BlockSpec((tm, tn), lambda i, j: (i, j))
        shape = jax.ShapeDtypeStruct((R, n_cols), F32)
        if out is None:
            out = pl.pallas_call(_mm_body, grid=(R // tm, n_cols // tn), in_specs=[x_spec, w_spec],
                                 out_specs=o_spec, out_shape=shape,
                                 compiler_params=_params(("parallel", "arbitrary")), name="proj")(x, w)
        else:
            out = pl.pallas_call(_mm_acc_body, grid=(R // tm, n_cols // tn), in_specs=[x_spec, w_spec, o_spec],
                                 out_specs=o_spec, out_shape=shape, input_output_aliases={2: 0},
                                 compiler_params=_params(("parallel", "arbitrary")), name="proj_acc")(x, w, out)
    return out


class _Rows:
    def __init__(self, kind, n_seq, n_chunks, rb, row_off=0):
        self.kind, self.n_seq, self.n_chunks, self.rb, self.row_off = kind, n_seq, n_chunks, rb, row_off

    def spec(self, width, col_fn):
        rb, nc, off = self.rb, self.n_chunks, self.row_off // self.rb
        if self.kind == "flat":
            return pl.BlockSpec((rb, width), lambda b, h, c: (off + b * nc + c, col_fn(h)))
        return pl.BlockSpec((None, rb, width), lambda b, h, c: (b, c, col_fn(h)))

    def prev_spec(self, width, col_fn):
        nc, k = self.n_chunks, self.rb // 8
        return pl.BlockSpec((8, width), lambda b, h, c: (jnp.maximum((b * nc + c) * k - 1, 0), col_fn(h)))

    def out_shape(self, width, dtype):
        if self.kind == "flat":
            return jax.ShapeDtypeStruct((self.n_seq * self.n_chunks * self.rb, width), dtype)
        return jax.ShapeDtypeStruct((self.n_seq, self.n_chunks * self.rb, width), dtype)

    def out_spec(self, width):
        rb, nc = self.rb, self.n_chunks
        if self.kind == "flat":
            return pl.BlockSpec((rb, width), lambda b, h, c: (b * nc + c, h))
        return pl.BlockSpec((None, rb, width), lambda b, h, c: (b, c, h))


def _lt(rb):
    return max(8, rb)


def _ssd_body(*refs, rb, has_prev):
    lt = _lt(rb)
    it = iter(refs)
    z_ref, x_ref, B_ref, C_ref = next(it), next(it), next(it), next(it)
    if has_prev:
        xp_ref, Bp_ref, Cp_ref = next(it), next(it), next(it)
    else:
        xp_ref = Bp_ref = Cp_ref = None
    csx_ref, csB_ref, csC_ref = next(it), next(it), next(it)
    cwx_ref, cwB_ref, cwC_ref = next(it), next(it), next(it)
    cbx_ref, cbB_ref, cbC_ref = next(it), next(it), next(it)
    dtc_ref, dtr_ref, hpr_ref, hpc_ref, D_ref, ng_ref, s0_ref = (next(it) for _ in range(7))
    y_ref, ocx_ref, ocB_ref, ocC_ref, sout_ref = (next(it) for _ in range(5))
    S_scr, ex_scr, eB_scr, eC_scr, xw_scr, y_scr = (next(it) for _ in range(6))
    if rb < CHUNK:
        xpad, Bpad, Cpad, dtcpad, dtrpad = (next(it) for _ in range(5))
    else:
        xpad = Bpad = Cpad = dtcpad = dtrpad = None

    c = pl.program_id(2)
    nc = pl.num_programs(2)

    @pl.when(c == 0)
    def _():
        S_scr[...] = s0_ref[...]

    def conv(cur_ref, prev_ref, cs_ref, w_ref, b_ref, ext, oc_ref):
        if has_prev:
            prev3 = jnp.where(c == 0, cs_ref[...], prev_ref[5:8, :])
        else:
            prev3 = cs_ref[...]
        ext[5:8, :] = prev3
        ext[8:8 + rb, :] = cur_ref[...]
        acc = b_ref[...] + ext[5:5 + rb, :] * w_ref[0:1, :]
        for k in range(1, SSD_CONV_W):
            acc = acc + ext[5 + k:5 + k + rb, :] * w_ref[k:k + 1, :]
        oc_ref[...] = ext[5 + rb:8 + rb, :]
        return _silu(acc)

    xs = _pad_rows(conv(x_ref, xp_ref, csx_ref, cwx_ref, cbx_ref, ex_scr, ocx_ref), xpad)
    Bs = _pad_rows(conv(B_ref, Bp_ref, csB_ref, cwB_ref, cbB_ref, eB_scr, ocB_ref), Bpad)
    Cs = _pad_rows(conv(C_ref, Cp_ref, csC_ref, cwC_ref, cbC_ref, eC_scr, ocC_ref), Cpad)

    hpr = hpr_ref[...]
    hpc = hpc_ref[...]
    dtc = _pad_rows(_softplus(dtc_ref[...] + hpr[0:1, :]), dtcpad)
    dtr = _pad_lanes(_softplus(dtr_ref[...] + hpc[:, 0:1]), dtrpad)
    a_c = dtc * (-jnp.exp(hpr[1:2, :]))
    a_r = dtr * (-jnp.exp(hpc[:, 1:2]))
    cum_c = _cum_left(_tri(CHUNK, True), a_c)
    cum_r = _cum_right(a_r, _tri(CHUNK, False))
    tot_r = jnp.sum(a_r, axis=1, keepdims=True)
    tot_c = cum_c[CHUNK - 1:CHUNK, :]
    e_c = jnp.exp(cum_c)
    te_c = jnp.exp(tot_c - cum_c) * dtc

    Bs_bf = Bs.astype(BF16)
    Ct_bf = Cs[0:lt, :].astype(BF16)
    cb = _dot_nt(Ct_bf, Bs_bf)
    Yi = _dot_nt(Ct_bf, S_scr[...].astype(BF16))

    row = lax.broadcasted_iota(jnp.int32, (lt, CHUNK), 0)
    col = lax.broadcasted_iota(jnp.int32, (lt, CHUNK), 1)
    causal = row >= col
    lo_t = col < SSD_HEAD_DIM
    lo_s = lax.broadcasted_iota(jnp.int32, (CHUNK, LANES), 1) < SSD_HEAD_DIM

    for jj in range(8):
        j0, j1 = 2 * jj, 2 * jj + 1
        sl = slice(LANES * jj, LANES * (jj + 1))
        ws = []
        for j in (j0, j1):
            seg = cum_c[0:lt, j:j + 1] - cum_r[j:j + 1, :]
            dec = jnp.exp(jnp.where(causal, seg, -jnp.inf))
            ws.append((dec * dtr[j:j + 1, :] * cb).astype(BF16))
        lhs = jnp.concatenate(ws, axis=1)
        xp = xs[:, sl]
        rhs = jnp.concatenate([jnp.where(lo_s, xp, 0.0).astype(BF16),
                               jnp.where(lo_s, 0.0, xp).astype(BF16)], axis=0)
        e_pair = jnp.where(lo_t, e_c[0:lt, j0:j0 + 1], e_c[0:lt, j1:j1 + 1])
        y_scr[:, sl] = _dot(lhs, rhs) + Yi[:, sl] * e_pair + xp[0:lt, :] * D_ref[:, sl]
        te_pair = jnp.where(lo_s, te_c[:, j0:j0 + 1], te_c[:, j1:j1 + 1])
        xw_scr[:, sl] = (xp * te_pair).astype(BF16)

    yg = y_scr[0:rb, :] * _silu(z_ref[...])
    yn = yg * lax.rsqrt(jnp.mean(yg * yg, axis=-1, keepdims=True) + EPS) * ng_ref[...]
    y_ref[...] = yn.astype(y_ref.dtype)

    upd = _dot_tn(xw_scr[...], Bs_bf)
    etot = jnp.broadcast_to(jnp.exp(tot_r), (16, LANES))
    for j in range(16):
        rs = slice(SSD_HEAD_DIM * j, SSD_HEAD_DIM * (j + 1))
        S_scr[rs, :] = S_scr[rs, :] * etot[j:j + 1, :] + upd[rs, :]

    @pl.when(c == nc - 1)
    def _():
        sout_ref[...] = S_scr[...]


def _ssd_call(rows, P, side, conv_in, ssm_in, conv_w, conv_b, dt_bias, A_log, d_skip, norm_g, y_dtype):
    n_seq, nc, rb = rows.n_seq, rows.n_chunks, rows.rb
    L = nc * rb
    G = SSD_GROUPS
    d_inner = d_skip.shape[0] * SSD_HEAD_DIM
    gw = d_inner // G
    n_state = ssm_in.shape[-1]
    hg = gw // SSD_HEAD_DIM
    shared = conv_in.shape[0] == 1 and n_seq > 1
    sb = (lambda b: 0) if shared else (lambda b: b)

    s4 = side.reshape(n_seq, L, G, hg)
    dtc = jnp.pad(jnp.transpose(s4, (2, 0, 1, 3)), ((0, 0), (0, 0), (0, 0), (0, LANES - hg)))
    dtr = jnp.transpose(s4, (2, 0, 3, 1))
    hp = jnp.stack([dt_bias.reshape(G, hg), A_log.reshape(G, hg)], axis=1)
    hpr = jnp.pad(hp, ((0, 0), (0, 0), (0, LANES - hg)))
    hpc = jnp.transpose(hp, (0, 2, 1))
    d_exp = jnp.repeat(d_skip, SSD_HEAD_DIM).reshape(1, d_inner)
    ng = norm_g.reshape(1, d_inner)
    cb2 = conv_b.reshape(1, -1)

    xo, Bo, Co = d_inner // gw, 2 * d_inner // LANES, 2 * d_inner // LANES + n_state * G // LANES
    z_c, x_c = (lambda g: g), (lambda g: xo + g)
    B_c, C_c = (lambda g: Bo + g), (lambda g: Co + g)
    cxo, cBo, cCo = (lambda g: g), (lambda g: d_inner // LANES + g), (lambda g: d_inner // LANES + G + g)
    has_prev = nc > 1

    ins = [P, P, P, P]
    specs = [rows.spec(gw, z_c), rows.spec(gw, x_c), rows.spec(LANES, B_c), rows.spec(LANES, C_c)]
    if has_prev:
        ins += [P, P, P]
        specs += [rows.prev_spec(gw, x_c), rows.prev_spec(LANES, B_c), rows.prev_spec(LANES, C_c)]
    ins += [conv_in, conv_in, conv_in, conv_w, conv_w, conv_w, cb2, cb2, cb2]
    specs += [pl.BlockSpec((None, 3, gw), lambda b, g, c: (sb(b), 0, cxo(g))),
              pl.BlockSpec((None, 3, LANES), lambda b, g, c: (sb(b), 0, cBo(g))),
              pl.BlockSpec((None, 3, LANES), lambda b, g, c: (sb(b), 0, cCo(g))),
              pl.BlockSpec((SSD_CONV_W, gw), lambda b, g, c: (0, cxo(g))),
              pl.BlockSpec((SSD_CONV_W, LANES), lambda b, g, c: (0, cBo(g))),
              pl.BlockSpec((SSD_CONV_W, LANES), lambda b, g, c: (0, cCo(g))),
              pl.BlockSpec((1, gw), lambda b, g, c: (0, cxo(g))),
              pl.BlockSpec((1, LANES), lambda b, g, c: (0, cBo(g))),
              pl.BlockSpec((1, LANES), lambda b, g, c: (0, cCo(g)))]
    ins += [dtc, dtr, hpr, hpc, d_exp, ng, ssm_in]
    specs += [pl.BlockSpec((None, None, rb, LANES), lambda b, g, c: (g, b, c, 0)),
              pl.BlockSpec((None, None, hg, rb), lambda b, g, c: (g, b, 0, c)),
              pl.BlockSpec((None, 2, LANES), lambda b, g, c: (g, 0, 0)),
              pl.BlockSpec((None, hg, 2), lambda b, g, c: (g, 0, 0)),
              pl.BlockSpec((1, gw), lambda b, g, c: (0, g)),
              pl.BlockSpec((1, gw), lambda b, g, c: (0, g)),
              pl.BlockSpec((None, gw, n_state), lambda b, g, c: (sb(b), g, 0))]
    out_shape = [rows.out_shape(d_inner, y_dtype),
                 jax.ShapeDtypeStruct((n_seq, 3, d_inner), F32),
                 jax.ShapeDtypeStruct((n_seq, 3, G * n_state), F32),
                 jax.ShapeDtypeStruct((n_seq, 3, G * n_state), F32),
                 jax.ShapeDtypeStruct((n_seq, d_inner, n_state), F32)]
    out_specs = [rows.out_spec(gw),
                 pl.BlockSpec((None, 3, gw), lambda b, g, c: (b, 0, g)),
                 pl.BlockSpec((None, 3, LANES), lambda b, g, c: (b, 0, g)),
                 pl.BlockSpec((None, 3, LANES), lambda b, g, c: (b, 0, g)),
                 pl.BlockSpec((None, gw, n_state), lambda b, g, c: (b, g, 0))]
    lt = _lt(rb)
    scratch = [pltpu.VMEM((gw, n_state), F32), pltpu.VMEM((8 + rb, gw), F32), pltpu.VMEM((8 + rb, LANES), F32),
               pltpu.VMEM((8 + rb, LANES), F32), pltpu.VMEM((CHUNK, gw), BF16), pltpu.VMEM((lt, gw), F32)]
    if rb < CHUNK:
        scratch += [pltpu.VMEM((CHUNK, gw), F32), pltpu.VMEM((CHUNK, LANES), F32), pltpu.VMEM((CHUNK, LANES), F32),
                    pltpu.VMEM((CHUNK, LANES), F32), pltpu.VMEM((hg, CHUNK), F32)]
    return pl.pallas_call(
        functools.partial(_ssd_body, rb=rb, has_prev=has_prev),
        grid=(n_seq, G, nc), in_specs=specs, out_specs=out_specs, out_shape=out_shape, scratch_shapes=scratch,
        compiler_params=_params(("parallel", "parallel", "arbitrary")), name="ssd")(*ins)


def _ml_body(*refs, rb, scale):
    lt = _lt(rb)
    it = iter(refs)
    q_ref, k_ref, v_ref, o_ref, z_ref, gc_ref, gr_ref, bgr_ref, bgc_ref, C0_ref, n0_ref, m0_ref, ng_ref = (
        next(it) for _ in range(13))
    h_ref, Cout_ref, nout_ref, mout_ref = (next(it) for _ in range(4))
    C_scr, n_scr, m_scr = next(it), next(it), next(it)
    if rb < CHUNK:
        kpad, vpad, qpad, igpad, lfpad, grpad = (next(it) for _ in range(6))
    else:
        kpad = vpad = qpad = igpad = lfpad = grpad = None

    c = pl.program_id(2)
    nc = pl.num_programs(2)

    @pl.when(c == 0)
    def _():
        C_scr[...] = C0_ref[...]
        n_scr[...] = n0_ref[...]
        m_scr[...] = m0_ref[...]

    gc = gc_ref[...] + bgr_ref[...]
    ig_c = _pad_rows(jnp.broadcast_to(gc[:, 0:1], (rb, LANES)), igpad, NEG_BIG)
    lf_c = _pad_rows(jnp.broadcast_to(_log_sigmoid(gc[:, 1:2]), (rb, LANES)), lfpad)
    gr = gr_ref[...] + bgc_ref[...]
    rsel = lax.broadcasted_iota(jnp.int32, gr.shape, 0)
    gr = jnp.where(rsel == 1, _log_sigmoid(gr), gr)
    if rb < CHUNK:
        grpad[...] = jnp.zeros(grpad.shape, F32)
        grpad[0:1, :] = jnp.full((1, CHUNK), NEG_BIG, F32)
        grpad[:, 0:rb] = gr
        gr = grpad[...]
    ig_r = gr[0:1, :]
    bcum_c = _cum_left(_tri(CHUNK, True), lf_c)
    bcum_r = _cum_right(gr, _tri(CHUNK, False))[1:2, :]

    row = lax.broadcasted_iota(jnp.int32, (lt, CHUNK), 0)
    col = lax.broadcasted_iota(jnp.int32, (lt, CHUNK), 1)
    logd = jnp.where(row >= col, bcum_c[0:lt, :] - bcum_r + ig_r, -jnp.inf)
    m_prev = m_scr[...]
    m_inter = bcum_c[0:lt, 0:1] + m_prev
    m_t = jnp.maximum(m_inter, jnp.max(logd, axis=1, keepdims=True))
    dmat = jnp.exp(logd - m_t)
    w_prev = jnp.exp(m_inter - m_t)

    q = _pad_rows(q_ref[...], qpad)[0:lt, :]
    ks = _pad_rows(k_ref[...], kpad) * scale
    vs_bf = _pad_rows(v_ref[...], vpad).astype(BF16)
    q_bf = q.astype(BF16)
    s = _dot_nt(q_bf, ks.astype(BF16)) * dmat
    num = _dot(s.astype(BF16), vs_bf) + w_prev * _dot(q_bf, C_scr[...].astype(BF16))
    den = jnp.sum(s, axis=1, keepdims=True) + w_prev * jnp.sum(q * n_scr[...], axis=1, keepdims=True)
    hout = num / jnp.maximum(jnp.abs(den), jnp.exp(-m_t))

    hc = hout[0:rb, :] * jax.nn.sigmoid(o_ref[...])
    hn = hc * lax.rsqrt(jnp.mean(hc * hc, axis=-1, keepdims=True) + EPS) * ng_ref[...] * _silu(z_ref[...])
    h_ref[...] = hn.astype(h_ref.dtype)

    m_new = m_t[rb - 1:rb, :]
    bc_last = bcum_c[rb - 1:rb, 0:1]
    w_end = jnp.exp(bc_last - bcum_c + ig_c - m_new)
    dec = jnp.exp(bc_last + m_prev - m_new)
    kw = ks * jnp.concatenate([w_end, w_end], axis=1)
    C_scr[...] = dec * C_scr[...] + _dot_tn(kw.astype(BF16), vs_bf)
    n_scr[...] = dec * n_scr[...] + jnp.sum(kw, axis=0, keepdims=True)
    m_scr[...] = m_new

    @pl.when(c == nc - 1)
    def _():
        Cout_ref[...] = C_scr[...]
        nout_ref[...] = n_scr[...]
        mout_ref[...] = m_scr[...]


def _ml_call(rows, P, side, C_in, n_in, m_in, b_gates, norm_g, y_dtype):
    n_seq, nc, rb = rows.n_seq, rows.n_chunks, rows.rb
    L = nc * rb
    H = ML_HEADS
    dk, dv = C_in.shape[2], C_in.shape[3]
    shared = C_in.shape[0] == 1 and n_seq > 1
    sb = (lambda b: 0) if shared else (lambda b: b)

    g4 = side[:, :2 * H].reshape(n_seq, L, 2, H)
    gc = jnp.pad(jnp.transpose(g4, (3, 0, 1, 2)), ((0, 0), (0, 0), (0, 0), (0, LANES - 2)))
    gr = jnp.pad(jnp.transpose(g4, (3, 0, 2, 1)), ((0, 0), (0, 0), (0, 6), (0, 0)))
    bg = jnp.transpose(b_gates.reshape(2, H))
    bgr = jnp.pad(bg, ((0, 0), (0, LANES - 2))).reshape(H, 1, LANES)
    bgc = jnp.pad(bg, ((0, 0), (0, 6))).reshape(H, 8, 1)
    ng = norm_g.reshape(1, H * dv)

    nq, nv = H * dk, H * dv
    q_c, k_c = (lambda h: h), (lambda h: nq // dk + h)
    v_c, o_c, z_c = (lambda h: 2 * nq // dv + h), (lambda h: (2 * nq + nv) // dv + h), (lambda h: (2 * nq + 2 * nv) // dv + h)
    ins = [P, P, P, P, P, gc, gr, bgr, bgc, C_in, n_in, m_in, ng]
    specs = [rows.spec(dk, q_c), rows.spec(dk, k_c), rows.spec(dv, v_c), rows.spec(dv, o_c), rows.spec(dv, z_c),
             pl.BlockSpec((None, None, rb, LANES), lambda b, h, c: (h, b, c, 0)),
             pl.BlockSpec((None, None, 8, rb), lambda b, h, c: (h, b, 0, c)),
             pl.BlockSpec((None, 1, LANES), lambda b, h, c: (h, 0, 0)),
             pl.BlockSpec((None, 8, 1), lambda b, h, c: (h, 0, 0)),
             pl.BlockSpec((None, None, dk, dv), lambda b, h, c: (sb(b), h, 0, 0)),
             pl.BlockSpec((None, None, 1, dk), lambda b, h, c: (sb(b), h, 0, 0)),
             pl.BlockSpec((None, None, 1, 1), lambda b, h, c: (sb(b), h, 0, 0)),
             pl.BlockSpec((1, dv), lambda b, h, c: (0, h))]
    out_shape = [rows.out_shape(nv, y_dtype),
                 jax.ShapeDtypeStruct((n_seq, H, dk, dv), F32),
                 jax.ShapeDtypeStruct((n_seq, H, 1, dk), F32),
                 jax.ShapeDtypeStruct((n_seq, H, 1, 1), F32)]
    out_specs = [rows.out_spec(dv),
                 pl.BlockSpec((None, None, dk, dv), lambda b, h, c: (b, h, 0, 0)),
                 pl.BlockSpec((None, None, 1, dk), lambda b, h, c: (b, h, 0, 0)),
                 pl.BlockSpec((None, None, 1, 1), lambda b, h, c: (b, h, 0, 0))]
    scratch = [pltpu.VMEM((dk, dv), F32), pltpu.VMEM((1, dk), F32), pltpu.VMEM((1, 1), F32)]
    if rb < CHUNK:
        scratch += [pltpu.VMEM((CHUNK, dk), F32), pltpu.VMEM((CHUNK, dv), F32), pltpu.VMEM((CHUNK, dk), F32),
                    pltpu.VMEM((CHUNK, LANES), F32), pltpu.VMEM((CHUNK, LANES), F32), pltpu.VMEM((8, CHUNK), F32)]
    return pl.pallas_call(
        functools.partial(_ml_body, rb=rb, scale=float(dk) ** -0.5),
        grid=(n_seq, H, nc), in_specs=specs, out_specs=out_specs, out_shape=out_shape, scratch_shapes=scratch,
        compiler_params=_params(("parallel", "parallel", "arbitrary")), name="mlstm")(*ins)


def _ret_body(*refs, rb, scale):
    lt = _lt(rb)
    it = iter(refs)
    q_ref, k_ref, v_ref, g_ref, cos_ref, sin_ref, lg_ref, S0_ref, ng_ref = (next(it) for _ in range(9))
    y_ref, Sout_ref = next(it), next(it)
    S_scr = next(it)
    if rb < CHUNK:
        qpad, kpad, vpad = next(it), next(it), next(it)
    else:
        qpad = kpad = vpad = None

    c = pl.program_id(2)
    nc = pl.num_programs(2)

    @pl.when(c == 0)
    def _():
        S_scr[...] = S0_ref[...]

    cos, sin = cos_ref[...], sin_ref[...]
    half = cos.shape[1]

    def rope(x):
        x1, x2 = x[:, :half], x[:, half:]
        return jnp.concatenate([x1 * cos - x2 * sin, x1 * sin + x2 * cos], axis=1)

    q_bf = _pad_rows(rope(q_ref[...]), qpad)[0:lt, :].astype(BF16)
    ks = _pad_rows(rope(k_ref[...]) * scale, kpad)
    vs_bf = _pad_rows(v_ref[...], vpad).astype(BF16)
    lg = lg_ref[...]

    row = lax.broadcasted_iota(jnp.int32, (lt, CHUNK), 0)
    col = lax.broadcasted_iota(jnp.int32, (lt, CHUNK), 1)
    rel = (row - col).astype(F32)
    dmat = jnp.exp(jnp.where(row >= col, rel * lg, -jnp.inf))
    a = _dot_nt(q_bf, ks.astype(BF16)) * dmat
    e_t = jnp.exp((row + 1).astype(F32) * lg)
    nrep = vs_bf.shape[1] // LANES
    y = _dot(a.astype(BF16), vs_bf) + _dot(q_bf, S_scr[...].astype(BF16)) * jnp.concatenate([e_t] * nrep, axis=1)

    yr = y[0:rb, :]
    yn = yr * lax.rsqrt(jnp.mean(yr * yr, axis=-1, keepdims=True) + EPS) * ng_ref[...] * _silu(g_ref[...])
    y_ref[...] = yn.astype(y_ref.dtype)

    srow = lax.broadcasted_iota(jnp.int32, (CHUNK, LANES), 0)
    w_end = jnp.where(srow < rb, jnp.exp((rb - 1 - srow).astype(F32) * lg), 0.0)
    kw = ks * jnp.concatenate([w_end] * (ks.shape[1] // LANES), axis=1)
    e_all = jnp.exp(float(rb) * lg)
    S_scr[...] = S_scr[...] * jnp.concatenate([e_all] * nrep, axis=1) + _dot_tn(kw.astype(BF16), vs_bf)

    @pl.when(c == nc - 1)
    def _():
        Sout_ref[...] = S_scr[...]


def _ret_call(rows, P, S_in, cos, sin, norm_g, y_dtype):
    n_seq, nc, rb = rows.n_seq, rows.n_chunks, rows.rb
    H = RET_HEADS
    dk, dv = S_in.shape[2], S_in.shape[3]
    shared = S_in.shape[0] == 1 and n_seq > 1
    sb = (lambda b: 0) if shared else (lambda b: b)
    lg = jnp.log1p(-jnp.exp2(-5.0 - jnp.arange(H, dtype=F32)))
    lg = jnp.broadcast_to(lg[:, None, None], (H, 1, LANES))
    ng = norm_g.reshape(1, H * dv)
    nq, nv = H * dk, H * dv
    q_c, k_c = (lambda h: h), (lambda h: nq // dk + h)
    v_c, g_c = (lambda h: 2 * nq // dv + h), (lambda h: (2 * nq + nv) // dv + h)
    ins = [P, P, P, P, cos, sin, lg, S_in, ng]
    specs = [rows.spec(dk, q_c), rows.spec(dk, k_c), rows.spec(dv, v_c), rows.spec(dv, g_c),
             pl.BlockSpec((rb, dk // 2), lambda b, h, c: (c, 0)),
             pl.BlockSpec((rb, dk // 2), lambda b, h, c: (c, 0)),
             pl.BlockSpec((None, 1, LANES), lambda b, h, c: (h, 0, 0)),
             pl.BlockSpec((None, None, dk, dv), lambda b, h, c: (sb(b), h, 0, 0)),
             pl.BlockSpec((1, dv), lambda b, h, c: (0, h))]
    out_shape = [rows.out_shape(nv, y_dtype), jax.ShapeDtypeStruct((n_seq, H, dk, dv), F32)]
    out_specs = [rows.out_spec(dv), pl.BlockSpec((None, None, dk, dv), lambda b, h, c: (b, h, 0, 0))]
    scratch = [pltpu.VMEM((dk, dv), F32)]
    if rb < CHUNK:
        scratch += [pltpu.VMEM((CHUNK, dk), F32), pltpu.VMEM((CHUNK, dk), F32), pltpu.VMEM((CHUNK, dv), F32)]
    return pl.pallas_call(
        functools.partial(_ret_body, rb=rb, scale=float(dk) ** -0.5),
        grid=(n_seq, H, nc), in_specs=specs, out_specs=out_specs, out_shape=out_shape, scratch_shapes=scratch,
        compiler_params=_params(("parallel", "parallel", "arbitrary")), name="retention")(*ins)


def _rope_tables(pos, half):
    inv = ROPE_BASE ** (-jnp.arange(half, dtype=F32) / half)
    ang = pos.astype(F32)[:, None] * inv
    return jnp.cos(ang), jnp.sin(ang)


def kernel(x_prompt, x_sample, state_ssm, state_ssd_conv, state_mlstm_C, state_mlstm_n, state_mlstm_m, state_ret, meta_tokens, pre_norm_g, post_norm_g, ssd_w_in, ssd_conv_w, ssd_conv_b, ssd_dt_bias, ssd_A_log, ssd_D, ssd_norm_g, ssd_w_out, ml_w_in, ml_b_gates, ml_norm_g, ml_w_out, ret_w_in, ret_norm_g, ret_w_out):
    Bp, Sp, D = x_prompt.shape
    Bd, Ld, _ = x_sample.shape
    n_meta = meta_tokens.shape[0]
    depth = pre_norm_g.shape[0]
    n_p, n_s = Bp * Sp, Bd * Ld
    assert Sp % CHUNK == 0 and (n_p + n_s) % n_meta == 0 and n_meta % 8 == 0 and Ld <= 8

    rows_p = _Rows("flat", Bp, Sp // CHUNK, CHUNK, 0)
    rows_s = _Rows("seq", Bd, 1, Ld)
    rows_m = _Rows("flat", 1, 1, n_meta, n_p + n_s)

    def parts(a):
        return a, a[n_p:n_p + n_s].reshape(Bd, Ld, a.shape[1]), a

    def join(yp, ys, ym):
        return jnp.concatenate([yp, ys.reshape(n_s, -1).astype(BF16), ym], axis=0)

    x = jnp.concatenate([x_prompt.reshape(n_p, D), x_sample.reshape(n_s, D), meta_tokens.astype(x_prompt.dtype)], axis=0)

    def side_w(w, lo, n):
        return jnp.pad(w[:, lo:lo + n], ((0, 0), (0, LANES - n)))

    H_ssd, P_ssd, N_ssd = state_ssm.shape[2:]
    d_inner = H_ssd * P_ssd
    conv_dim = state_ssd_conv.shape[-1]
    ssd_main = d_inner + conv_dim
    ml_dk, ml_dv = state_mlstm_C.shape[3:]
    ml_main = 2 * ML_HEADS * ml_dk + 3 * ML_HEADS * ml_dv
    ret_dk, ret_dv = state_ret.shape[3:]
    ret_main = 2 * RET_HEADS * ret_dk + 2 * RET_HEADS * ret_dv

    def side_for(i):
        kind, j = i % 3, i // 3
        if kind == 0:
            return side_w(ssd_w_in[j], ssd_main, H_ssd)
        if kind == 1:
            return side_w(ml_w_in[j], ml_main, 2 * ML_HEADS)
        return None

    cos_p, sin_p = _rope_tables(n_meta + jnp.arange(Sp), ret_dk // 2)
    cos_s, sin_s = _rope_tables(PAST_LEN + jnp.arange(Ld), ret_dk // 2)
    cos_m, sin_m = _rope_tables(jnp.arange(n_meta), ret_dk // 2)

    ssm_p, conv_p, mC_p, mn_p, mm_p, ret_p = [], [], [], [], [], []
    ssm_s, conv_s, mC_s, mn_s, mm_s, ret_s = [], [], [], [], [], []

    sw = side_for(0)
    h, side = _norm_call(x, g_pre=pre_norm_g[0], w_side=sw)
    for i in range(depth):
        kind, j = i % 3, i // 3
        if kind == 0:
            P = _matmul(h, ssd_w_in[j], ssd_main)
            P_p, P_s, P_m = parts(P)
            args = (ssd_conv_w[j], ssd_conv_b[j], ssd_dt_bias[j], ssd_A_log[j], ssd_D[j], ssd_norm_g[j])
            zc = jnp.zeros((1, SSD_CONV_W - 1, conv_dim), F32)
            zs = jnp.zeros((1, d_inner, N_ssd), F32)
            ym, cx, cB, cC, sm = _ssd_call(rows_m, P_m, side[n_p + n_s:], zc, zs, *args, BF16)
            conv_m = jnp.concatenate([cx, cB, cC], axis=-1)
            yp, cx, cB, cC, sp = _ssd_call(rows_p, P_p, side[:n_p], conv_m, sm, *args, BF16)
            conv_p.append(jnp.concatenate([cx, cB, cC], axis=-1))
            ssm_p.append(sp.reshape(Bp, H_ssd, P_ssd, N_ssd))
            ys, cx, cB, cC, ss = _ssd_call(rows_s, P_s, side[n_p:n_p + n_s], state_ssd_conv[j],
                                           state_ssm[j].reshape(Bd, d_inner, N_ssd), *args, F32)
            conv_s.append(jnp.concatenate([cx, cB, cC], axis=-1))
            ssm_s.append(ss.reshape(Bd, H_ssd, P_ssd, N_ssd))
            out = _matmul(join(yp, ys, ym), ssd_w_out[j], D)
        elif kind == 1:
            P = _matmul(h, ml_w_in[j], ml_main)
            P_p, P_s, P_m = parts(P)
            args = (ml_b_gates[j], ml_norm_g[j])
            z = lambda *s: jnp.zeros(s, F32)
            ym, Cm, nm, mm = _ml_call(rows_m, P_m, side[n_p + n_s:], z(1, ML_HEADS, ml_dk, ml_dv),
                                      z(1, ML_HEADS, 1, ml_dk), z(1, ML_HEADS, 1, 1), *args, BF16)
            yp, Cp, np_, mp = _ml_call(rows_p, P_p, side[:n_p], Cm, nm, mm, *args, BF16)
            mC_p.append(Cp)
            mn_p.append(np_.reshape(Bp, ML_HEADS, ml_dk))
            mm_p.append(mp.reshape(Bp, ML_HEADS))
            ys, Cs, ns_, ms = _ml_call(rows_s, P_s, side[n_p:n_p + n_s], state_mlstm_C[j],
                                       state_mlstm_n[j].reshape(Bd, ML_HEADS, 1, ml_dk),
                                       state_mlstm_m[j].reshape(Bd, ML_HEADS, 1, 1), *args, F32)
            mC_s.append(Cs)
            mn_s.append(ns_.reshape(Bd, ML_HEADS, ml_dk))
            mm_s.append(ms.reshape(Bd, ML_HEADS))
            out = _matmul(join(yp, ys, ym), ml_w_out[j], D)
        else:
            P = _matmul(h, ret_w_in[j], ret_main)
            P_p, P_s, P_m = parts(P)
            ym, Sm = _ret_call(rows_m, P_m, jnp.zeros((1, RET_HEADS, ret_dk, ret_dv), F32), cos_m, sin_m,
                               ret_norm_g[j], BF16)
            yp, Sp_ = _ret_call(rows_p, P_p, Sm, cos_p, sin_p, ret_norm_g[j], BF16)
            ret_p.append(Sp_)
            ys, Ss = _ret_call(rows_s, P_s, state_ret[j], cos_s, sin_s, ret_norm_g[j], F32)
            ret_s.append(Ss)
            out = _matmul(join(yp, ys, ym), ret_w_out[j], D)
        if i + 1 < depth:
            sw = side_for(i + 1)
            res = _norm_call(x, o=out, g_post=post_norm_g[i], g_pre=pre_norm_g[i + 1], w_side=sw)
            x, h = res[0], res[1]
            side = res[2] if sw is not None else None
        else:
            (x,) = _norm_call(x, o=out, g_post=post_norm_g[i])

    y_prompt = x[:n_p].reshape(Bp, Sp, D)
    y_sample = x[n_p:n_p + n_s].reshape(Bd, Ld, D)
    st = jnp.stack
    return (y_prompt, y_sample, st(ssm_p), st(conv_p), st(mC_p), st(mn_p), st(mm_p), st(ret_p),
            st(ssm_s), st(conv_s), st(mC_s), st(mn_s), st(mm_s), st(ret_s))
```

```python
import functools

import jax
import jax.numpy as jnp
from jax import lax
from jax.experimental import pallas as pl
from jax.experimental.pallas import tpu as pltpu

F32 = jnp.float32
BF16 = jnp.bfloat16

EPS = 1e-6
CHUNK = 128
LANES = 128
SUBLANES = 8
PAST_LEN = 16384
ROPE_BASE = 10000.0
K_BLOCK = 4096
NEG_BIG = -1e30
LOG2E = 1.4426950408889634
VMEM_LIMIT = 56 * 1024 * 1024

SSD_GROUPS = 8
SSD_HEAD_DIM = 64
SSD_CONV_W = 4
ML_HEADS = 8
RET_HEADS = 16


def _dot(a, b):
    return jnp.dot(a, b, preferred_element_type=F32)


def _dot_nt(a, b):
    return lax.dot_general(a, b, (((1,), (1,)), ((), ())), preferred_element_type=F32)


def _dot_tn(a, b):
    return lax.dot_general(a, b, (((0,), (0,)), ((), ())), preferred_element_type=F32)


def _split3(x):
    hi = x.astype(BF16)
    r1 = x - hi.astype(F32)
    mid = r1.astype(BF16)
    lo = (r1 - mid.astype(F32)).astype(BF16)
    return hi, mid, lo


def _cum_left(tri, x):
    hi, mid, lo = _split3(x)
    return _dot(tri, hi) + _dot(tri, mid) + _dot(tri, lo)


def _cum_right(x, tri):
    hi, mid, lo = _split3(x)
    return _dot(hi, tri) + _dot(mid, tri) + _dot(lo, tri)


def _tri(n, lower):
    r = lax.broadcasted_iota(jnp.int32, (n, n), 0)
    c = lax.broadcasted_iota(jnp.int32, (n, n), 1)
    m = (r >= c) if lower else (r <= c)
    return jnp.where(m, 1.0, 0.0).astype(BF16)


def _softplus(x):
    return jnp.maximum(x, 0.0) + jnp.log1p(jnp.exp(-jnp.abs(x)))


def _log_sigmoid(x):
    return jnp.minimum(x, 0.0) - jnp.log1p(jnp.exp(-jnp.abs(x)))


def _sigmoid(x):
    return 0.5 + 0.5 * jnp.tanh(0.5 * x)


def _silu(x):
    h = 0.5 * x
    return h + h * jnp.tanh(h)


def _pad_rows(val, pad_ref, fill=0.0):
    if pad_ref is None:
        return val
    rb = val.shape[0]
    pad_ref[...] = jnp.full(pad_ref.shape, fill, pad_ref.dtype)
    pad_ref[0:rb, :] = val
    return pad_ref[...]


def _pad_lanes(val, pad_ref, fill=0.0):
    if pad_ref is None:
        return val
    rb = val.shape[1]
    pad_ref[...] = jnp.full(pad_ref.shape, fill, pad_ref.dtype)
    pad_ref[:, 0:rb] = val
    return pad_ref[...]


def _row_tile(n, cap):
    best = None
    for t in range(16, min(n, cap) + 1, 16):
        if n % t == 0:
            best = t
    return best if best is not None else n


def _params(sem, limit=VMEM_LIMIT):
    return pltpu.CompilerParams(dimension_semantics=sem, vmem_limit_bytes=limit)


def _stack(parts):
    return parts[0][None] if len(parts) == 1 else jnp.stack(parts)


def _norm_body(*refs, has_post, has_pre, has_side):
    refs = list(refs)
    if has_post:
        o_ref, x_ref, gpost_ref = refs[:3]
        refs = refs[3:]
    else:
        x_ref = refs[0]
        refs = refs[1:]
    if has_pre:
        gpre_ref = refs[0]
        refs = refs[1:]
    if has_side:
        ws_ref = refs[0]
        refs = refs[1:]
    outs = refs
    x = x_ref[...]
    if has_post:
        o = o_ref[...]
        x = x + o * lax.rsqrt(jnp.mean(o * o, axis=-1, keepdims=True) + EPS) * gpost_ref[...]
        outs[0][...] = x
        outs = outs[1:]
    if has_pre:
        h = (x * lax.rsqrt(jnp.mean(x * x, axis=-1, keepdims=True) + EPS) * gpre_ref[...]).astype(BF16)
        outs[0][...] = h
        if has_side:
            outs[1][...] = _dot(h, ws_ref[...].astype(BF16))


def _norm_call(x, o=None, g_post=None, g_pre=None, w_side=None):
    R, D = x.shape
    has_post, has_pre, has_side = o is not None, g_pre is not None, w_side is not None
    tr = _row_tile(R, 256)
    row = pl.BlockSpec((tr, D), lambda i: (i, 0))
    vec = pl.BlockSpec((1, D), lambda i: (0, 0))
    ins, specs = [], []
    if has_post:
        ins += [o, x, g_post.reshape(1, D)]
        specs += [row, row, vec]
    else:
        ins += [x]
        specs += [row]
    if has_pre:
        ins += [g_pre.reshape(1, D)]
        specs += [vec]
    if has_side:
        ins += [w_side]
        specs += [pl.BlockSpec(w_side.shape, lambda i: (0, 0))]
    out_shape, out_specs = [], []
    if has_post:
        out_shape.append(jax.ShapeDtypeStruct((R, D), F32))
        out_specs.append(row)
    if has_pre:
        out_shape.append(jax.ShapeDtypeStruct((R, D), BF16))
        out_specs.append(row)
        if has_side:
            out_shape.append(jax.ShapeDtypeStruct((R, w_side.shape[1]), F32))
            out_specs.append(pl.BlockSpec((tr, w_side.shape[1]), lambda i: (i, 0)))
    return pl.pallas_call(
        functools.partial(_norm_body, has_post=has_post, has_pre=has_pre, has_side=has_side),
        grid=(R // tr,), in_specs=specs, out_specs=out_specs, out_shape=out_shape,
        compiler_params=_params(("parallel",)), name="norm")(*ins)


def _mm_body(x_ref, w_ref, o_ref):
    o_ref[...] = _dot(x_ref[...], w_ref[...].astype(BF16))


def _mm_acc_body(x_ref, w_ref, a_ref, o_ref):
    o_ref[...] = a_ref[...] + _dot(x_ref[...], w_ref[...].astype(BF16))


def _matmul(x, w, layer, n_cols):
    R, K = x.shape
    tm = _row_tile(R, 2048)
    tn = 512
    out = None
    for kb in range(K // K_BLOCK):
        x_spec = pl.BlockSpec((tm, K_BLOCK), lambda i, j, kb=kb: (i, kb), pipeline_mode=pl.Buffered(1))
        w_spec = pl.BlockSpec((None, K_BLOCK, tn), lambda i, j, kb=kb: (layer, kb, j))
        o_spec = pl.BlockSpec((tm, tn), lambda i, j: (i, j))
        shape = jax.ShapeDtypeStruct((R, n_cols), F32)
        if out is None:
            out = pl.pallas_call(_mm_body, grid=(R // tm, n_cols // tn), in_specs=[x_spec, w_spec],
                                 out_specs=o_spec, out_shape=shape,
                                 compiler_params=_params(("parallel", "arbitrary")), name="proj")(x, w)
        else:
            out = pl.pallas_call(_mm_acc_body, grid=(R // tm, n_cols // tn), in_specs=[x_spec, w_spec, o_spec],
                                 out_specs=o_spec, out_shape=shape, input_output_aliases={2: 0},
                                 compiler_params=_params(("parallel", "arbitrary")), name="proj_acc")(x, w, out)
    return out


class _Rows:
    def __init__(self, kind, n_seq, n_chunks, rb, row_off=0, y_rows=None):
        self.kind, self.n_seq, self.n_chunks, self.rb, self.row_off = kind, n_seq, n_chunks, rb, row_off
        self.y_rows = y_rows

    def spec(self, width, col_fn):
        rb, nc, off = self.rb, self.n_chunks, self.row_off // self.rb
        if self.kind == "flat":
            return pl.BlockSpec((rb, width), lambda b, h, c: (off + b * nc + c, col_fn(h)))
        return pl.BlockSpec((None, rb, width), lambda b, h, c: (b, c, col_fn(h)))

    def prev_spec(self, width, col_fn):
        nc, k = self.n_chunks, self.rb // SUBLANES
        return pl.BlockSpec((SUBLANES, width), lambda b, h, c: (jnp.maximum((b * nc + c) * k - 1, 0), col_fn(h)))

    def out_shape(self, width, dtype):
        if self.kind == "flat":
            rows = self.y_rows if self.y_rows is not None else self.n_seq * self.n_chunks * self.rb
            return jax.ShapeDtypeStruct((rows, width), dtype)
        return jax.ShapeDtypeStruct((self.n_seq, self.n_chunks * self.rb, width), dtype)

    def out_spec(self, width):
        rb, nc = self.rb, self.n_chunks
        if self.kind == "flat":
            return pl.BlockSpec((rb, width), lambda b, h, c: (b * nc + c, h))
        return pl.BlockSpec((None, rb, width), lambda b, h, c: (b, c, h))


def _ls(rb):
    return max(SUBLANES, rb)


def _ssd_body(*refs, rb, has_prev, gb, gw, has_alias):
    ls = _ls(rb)
    single = not has_prev
    it = iter(refs)
    z_ref, x_ref, B_ref, C_ref = next(it), next(it), next(it), next(it)
    if has_prev:
        xp_ref, Bp_ref, Cp_ref = next(it), next(it), next(it)
    else:
        xp_ref = Bp_ref = Cp_ref = None
    csx_ref, csB_ref, csC_ref = next(it), next(it), next(it)
    cwx_ref, cwB_ref, cwC_ref = next(it), next(it), next(it)
    cbx_ref, cbB_ref, cbC_ref = next(it), next(it), next(it)
    dtc_ref, dtr_ref, hpr_ref, hpc_ref, D_ref, ng_ref, s0_ref = (next(it) for _ in range(7))
    if has_alias:
        next(it)
    y_ref, ocx_ref, ocB_ref, ocC_ref, sout_ref = (next(it) for _ in range(5))
    S_scr = None if single else next(it)
    ex_scr, eB_scr, eC_scr, xw_scr, y_scr = (next(it) for _ in range(5))
    if rb < ls:
        xpad, Bpad, Cpad, dtcpad, dtrpad = (next(it) for _ in range(5))
    else:
        xpad = Bpad = Cpad = dtcpad = dtrpad = None

    c = pl.program_id(2)
    nc = pl.num_programs(2)

    if not single:
        @pl.when(c == 0)
        def _():
            S_scr[...] = s0_ref[...]
    S_old = s0_ref if single else S_scr
    S_new = sout_ref if single else S_scr

    def conv(cur_ref, prev_ref, cs_ref, w_ref, b_ref, ext, oc_ref):
        if has_prev:
            prev3 = jnp.where(c == 0, cs_ref[...], prev_ref[5:8, :])
        else:
            prev3 = cs_ref[...]
        ext[5:8, :] = prev3
        ext[8:8 + rb, :] = cur_ref[...]
        acc = b_ref[...] + ext[5:5 + rb, :] * w_ref[0:1, :]
        for k in range(1, SSD_CONV_W):
            acc = acc + ext[5 + k:5 + k + rb, :] * w_ref[k:k + 1, :]
        oc_ref[...] = ext[5 + rb:8 + rb, :]
        return _silu(acc)

    x_all = conv(x_ref, xp_ref, csx_ref, cwx_ref, cbx_ref, ex_scr, ocx_ref)
    B_all = conv(B_ref, Bp_ref, csB_ref, cwB_ref, cbB_ref, eB_scr, ocB_ref)
    C_all = conv(C_ref, Cp_ref, csC_ref, cwC_ref, cbC_ref, eC_scr, ocC_ref)

    row = lax.broadcasted_iota(jnp.int32, (ls, ls), 0)
    col = lax.broadcasted_iota(jnp.int32, (ls, ls), 1)
    causal = row >= col
    lo_s = lax.broadcasted_iota(jnp.int32, (ls, LANES), 1) < SSD_HEAD_DIM
    tri_l, tri_u = _tri(ls, True), _tri(ls, False)
    hg = gw // SSD_HEAD_DIM

    for u in range(gb):
        gs = slice(u * gw, (u + 1) * gw)
        ns = slice(u * LANES, (u + 1) * LANES)
        xs = _pad_rows(x_all[:, gs], xpad)
        Bs = _pad_rows(B_all[:, ns], Bpad)
        Cs = _pad_rows(C_all[:, ns], Cpad)

        hpr = hpr_ref[u]
        hpc = hpc_ref[u]
        dtc = _pad_rows(_softplus(dtc_ref[u] + hpr[0:1, :]), dtcpad)
        dtr = _pad_lanes(_softplus(dtr_ref[u] + hpc[:, 0:1]), dtrpad)
        a_c = dtc * (-LOG2E * jnp.exp(hpr[1:2, :]))
        a_r = dtr * (-LOG2E * jnp.exp(hpc[:, 1:2]))
        cum_c = _cum_left(tri_l, a_c)
        cum_r = _cum_right(a_r, tri_u)
        tot_r = jnp.sum(a_r, axis=1, keepdims=True)
        tot_c = cum_c[ls - 1:ls, :]
        e_c = jnp.exp2(cum_c)
        te_c = jnp.exp2(tot_c - cum_c) * dtc
        r2 = cum_r - jnp.log2(dtr)

        Bs_bf = Bs.astype(BF16)
        Cs_bf = Cs.astype(BF16)
        cb = _dot_nt(Cs_bf, Bs_bf)
        Yi = _dot_nt(Cs_bf, S_old[gs, :].astype(BF16))

        for jj in range(hg // 2):
            j0, j1 = 2 * jj, 2 * jj + 1
            sl = slice(LANES * jj, LANES * (jj + 1))
            ws = []
            for j in (j0, j1):
                seg = cum_c[:, j:j + 1] - r2[j:j + 1, :]
                ws.append((jnp.exp2(jnp.where(causal, seg, -jnp.inf)) * cb).astype(BF16))
            xp = xs[:, sl]
            x_lo = jnp.where(lo_s, xp, 0.0).astype(BF16)
            x_hi = jnp.where(lo_s, 0.0, xp).astype(BF16)
            if ls == CHUNK:
                intra = _dot(jnp.concatenate(ws, axis=1), jnp.concatenate([x_lo, x_hi], axis=0))
            else:
                intra = _dot(ws[0], x_lo) + _dot(ws[1], x_hi)
            e_pair = jnp.where(lo_s, e_c[:, j0:j0 + 1], e_c[:, j1:j1 + 1])
            y_scr[:, sl] = intra + Yi[:, sl] * e_pair + xp * D_ref[:, u * gw + LANES * jj:u * gw + LANES * (jj + 1)]
            te_pair = jnp.where(lo_s, te_c[:, j0:j0 + 1], te_c[:, j1:j1 + 1])
            xw_scr[:, sl] = (xp * te_pair).astype(xw_scr.dtype)

        yg = y_scr[0:rb, :] * _silu(z_ref[:, gs])
        yn = yg * lax.rsqrt(jnp.mean(yg * yg, axis=-1, keepdims=True) + EPS) * ng_ref[:, gs]
        y_ref[:, gs] = yn.astype(y_ref.dtype)

        upd = _dot_tn(xw_scr[...].astype(BF16), Bs_bf)
        etot = jnp.broadcast_to(jnp.exp2(tot_r), (hg, LANES))
        for j in range(hg):
            rs = slice(u * gw + SSD_HEAD_DIM * j, u * gw + SSD_HEAD_DIM * (j + 1))
            us = slice(SSD_HEAD_DIM * j, SSD_HEAD_DIM * (j + 1))
            S_new[rs, :] = S_old[rs, :] * etot[j:j + 1, :] + upd[us, :]

    if not single:
        @pl.when(c == nc - 1)
        def _():
            sout_ref[...] = S_scr[...]


def _ssd_call(rows, gb, P, side, conv_in, ssm_in, layer, conv_w, conv_b, dt_bias, A_log, d_skip, norm_g, y_dtype,
              ssm_alias=None, n_layers_out=None):
    n_seq, nc, rb = rows.n_seq, rows.n_chunks, rows.rb
    L = nc * rb
    G = SSD_GROUPS
    d_inner = d_skip.shape[0] * SSD_HEAD_DIM
    gw = d_inner // G
    n_state = ssm_in.shape[-1]
    hg = gw // SSD_HEAD_DIM
    shared = conv_in.shape[1] == 1 and n_seq > 1
    sb = (lambda b: 0) if shared else (lambda b: b)
    assert G % gb == 0 and n_state == LANES

    s4 = side.reshape(n_seq, L, G, hg)
    dtc = jnp.pad(jnp.transpose(s4, (2, 0, 1, 3)), ((0, 0), (0, 0), (0, 0), (0, LANES - hg)))
    dtr = jnp.transpose(s4, (2, 0, 3, 1))
    hp = jnp.stack([dt_bias.reshape(G, hg), A_log.reshape(G, hg)], axis=1)
    hpr = jnp.pad(hp, ((0, 0), (0, 0), (0, LANES - hg)))
    hpc = jnp.transpose(hp, (0, 2, 1))
    d_exp = jnp.repeat(d_skip, SSD_HEAD_DIM).reshape(1, d_inner)
    ng = norm_g.reshape(1, d_inner)
    cb2 = conv_b.reshape(1, -1)

    wx, wn = gb * gw, gb * LANES
    xo, Bo, Co = d_inner // wx, 2 * d_inner // wn, (2 * d_inner + G * n_state) // wn
    z_c, x_c = (lambda g: g), (lambda g: xo + g)
    B_c, C_c = (lambda g: Bo + g), (lambda g: Co + g)
    cxo, cBo, cCo = (lambda g: g), (lambda g: d_inner // wn + g), (lambda g: (d_inner + G * n_state) // wn + g)
    has_prev = nc > 1

    ins = [P, P, P, P]
    specs = [rows.spec(wx, z_c), rows.spec(wx, x_c), rows.spec(wn, B_c), rows.spec(wn, C_c)]
    if has_prev:
        ins += [P, P, P]
        specs += [rows.prev_spec(wx, x_c), rows.prev_spec(wn, B_c), rows.prev_spec(wn, C_c)]
    ins += [conv_in, conv_in, conv_in, conv_w, conv_w, conv_w, cb2, cb2, cb2]
    specs += [pl.BlockSpec((None, None, 3, wx), lambda b, g, c: (layer, sb(b), 0, cxo(g))),
              pl.BlockSpec((None, None, 3, wn), lambda b, g, c: (layer, sb(b), 0, cBo(g))),
              pl.BlockSpec((None, None, 3, wn), lambda b, g, c: (layer, sb(b), 0, cCo(g))),
              pl.BlockSpec((SSD_CONV_W, wx), lambda b, g, c: (0, cxo(g))),
              pl.BlockSpec((SSD_CONV_W, wn), lambda b, g, c: (0, cBo(g))),
              pl.BlockSpec((SSD_CONV_W, wn), lambda b, g, c: (0, cCo(g))),
              pl.BlockSpec((1, wx), lambda b, g, c: (0, cxo(g))),
              pl.BlockSpec((1, wn), lambda b, g, c: (0, cBo(g))),
              pl.BlockSpec((1, wn), lambda b, g, c: (0, cCo(g)))]
    ins += [dtc, dtr, hpr, hpc, d_exp, ng, ssm_in]
    specs += [pl.BlockSpec((gb, None, rb, LANES), lambda b, g, c: (g, b, c, 0)),
              pl.BlockSpec((gb, None, hg, rb), lambda b, g, c: (g, b, 0, c)),
              pl.BlockSpec((gb, 2, LANES), lambda b, g, c: (g, 0, 0)),
              pl.BlockSpec((gb, hg, 2), lambda b, g, c: (g, 0, 0)),
              pl.BlockSpec((1, wx), lambda b, g, c: (0, g)),
              pl.BlockSpec((1, wx), lambda b, g, c: (0, g)),
              pl.BlockSpec((None, None, wx, n_state), lambda b, g, c: (layer, sb(b), g, 0))]
    aliases = {}
    if ssm_alias is not None:
        aliases = {len(ins): 4}
        ins += [ssm_alias]
        specs += [pl.BlockSpec(memory_space=pl.ANY)]
    if n_layers_out is None:
        s_shape = jax.ShapeDtypeStruct((n_seq, d_inner, n_state), F32)
        s_spec = pl.BlockSpec((None, wx, n_state), lambda b, g, c: (b, g, 0))
    else:
        s_shape = jax.ShapeDtypeStruct((n_layers_out, n_seq, d_inner, n_state), F32)
        s_spec = pl.BlockSpec((None, None, wx, n_state), lambda b, g, c: (layer, b, g, 0))
    out_shape = [rows.out_shape(d_inner, y_dtype),
                 jax.ShapeDtypeStruct((n_seq, 3, d_inner), F32),
                 jax.ShapeDtypeStruct((n_seq, 3, G * n_state), F32),
                 jax.ShapeDtypeStruct((n_seq, 3, G * n_state), F32),
                 s_shape]
    out_specs = [rows.out_spec(wx),
                 pl.BlockSpec((None, 3, wx), lambda b, g, c: (b, 0, g)),
                 pl.BlockSpec((None, 3, wn), lambda b, g, c: (b, 0, g)),
                 pl.BlockSpec((None, 3, wn), lambda b, g, c: (b, 0, g)),
                 s_spec]
    ls = _ls(rb)
    scratch = [] if not has_prev else [pltpu.VMEM((wx, n_state), F32)]
    scratch += [pltpu.VMEM((8 + rb, wx), F32), pltpu.VMEM((8 + rb, wn), F32), pltpu.VMEM((8 + rb, wn), F32),
                pltpu.VMEM((ls, gw), BF16 if ls == CHUNK else F32), pltpu.VMEM((ls, gw), F32)]
    if rb < ls:
        scratch += [pltpu.VMEM((ls, gw), F32), pltpu.VMEM((ls, LANES), F32), pltpu.VMEM((ls, LANES), F32),
                    pltpu.VMEM((ls, LANES), F32), pltpu.VMEM((hg, ls), F32)]
    return pl.pallas_call(
        functools.partial(_ssd_body, rb=rb, has_prev=has_prev, gb=gb, gw=gw, has_alias=ssm_alias is not None),
        grid=(n_seq, G // gb, nc), in_specs=specs, out_specs=out_specs, out_shape=out_shape, scratch_shapes=scratch,
        input_output_aliases=aliases,
        compiler_params=_params(("parallel", "parallel", "arbitrary")), name="ssd")(*ins)


def _ml_body(*refs, rb, hb, dk, dv, scale, single):
    ls = _ls(rb)
    it = iter(refs)
    q_ref, k_ref, v_ref, o_ref, z_ref, gc_ref, gr_ref, bgr_ref, bgc_ref, C0_ref, n0_ref, m0_ref, ng_ref = (
        next(it) for _ in range(13))
    h_ref, Cout_ref, nout_ref, mout_ref = (next(it) for _ in range(4))
    if single:
        C_old, n_old, m_old, C_new, n_new, m_new_ref = C0_ref, n0_ref, m0_ref, Cout_ref, nout_ref, mout_ref
    else:
        C_scr, n_scr, m_scr = next(it), next(it), next(it)
        C_old, n_old, m_old, C_new, n_new, m_new_ref = C_scr, n_scr, m_scr, C_scr, n_scr, m_scr
    if rb < ls:
        kpad, vpad, qpad, igpad, lfpad, grpad = (next(it) for _ in range(6))
    else:
        kpad = vpad = qpad = igpad = lfpad = grpad = None

    c = pl.program_id(2)
    nc = pl.num_programs(2)

    if not single:
        @pl.when(c == 0)
        def _():
            C_scr[...] = C0_ref[...]
            n_scr[...] = n0_ref[...]
            m_scr[...] = m0_ref[...]

    row = lax.broadcasted_iota(jnp.int32, (ls, ls), 0)
    col = lax.broadcasted_iota(jnp.int32, (ls, ls), 1)
    causal = row >= col
    tri_l, tri_u = _tri(ls, True), _tri(ls, False)

    for u in range(hb):
        ks_ = slice(u * dk, (u + 1) * dk)
        vs_ = slice(u * dv, (u + 1) * dv)
        gc = gc_ref[u] + bgr_ref[u]
        ig_c = _pad_rows(jnp.broadcast_to(gc[:, 0:1], (rb, LANES)), igpad, NEG_BIG)
        lf_c = _pad_rows(jnp.broadcast_to(_log_sigmoid(gc[:, 1:2]), (rb, LANES)), lfpad)
        gr = gr_ref[u] + bgc_ref[u]
        rsel = lax.broadcasted_iota(jnp.int32, gr.shape, 0)
        gr = jnp.where(rsel == 1, _log_sigmoid(gr), gr)
        if grpad is not None:
            grpad[...] = jnp.zeros(grpad.shape, F32)
            grpad[0:1, :] = jnp.full((1, ls), NEG_BIG, F32)
            grpad[:, 0:rb] = gr
            gr = grpad[...]
        ig_r = gr[0:1, :]
        bcum_c = _cum_left(tri_l, lf_c)
        bcum_r = _cum_right(gr, tri_u)[1:2, :]

        logd = jnp.where(causal, bcum_c[:, 0:ls] - bcum_r + ig_r, -jnp.inf)
        m_prev = m_old[u]
        m_inter = bcum_c[:, 0:1] + m_prev
        m_t = jnp.maximum(m_inter, jnp.max(logd, axis=1, keepdims=True))
        dmat = jnp.exp(logd - m_t)
        w_prev = jnp.exp(m_inter - m_t)

        q = _pad_rows(q_ref[:, ks_], qpad)
        ks = _pad_rows(k_ref[:, ks_], kpad) * scale
        vs_bf = _pad_rows(v_ref[:, vs_], vpad).astype(BF16)
        q_bf = q.astype(BF16)
        Cu = C_old[u]
        s = _dot_nt(q_bf, ks.astype(BF16)) * dmat
        num = _dot(s.astype(BF16), vs_bf) + w_prev * _dot(q_bf, Cu.astype(BF16))
        den = jnp.sum(s, axis=1, keepdims=True) + w_prev * jnp.sum(q * n_old[u], axis=1, keepdims=True)
        hout = num / jnp.maximum(jnp.abs(den), jnp.exp(-m_t))

        hc = hout[0:rb, :] * _sigmoid(o_ref[:, vs_])
        hn = hc * lax.rsqrt(jnp.mean(hc * hc, axis=-1, keepdims=True) + EPS) * ng_ref[:, vs_] * _silu(z_ref[:, vs_])
        h_ref[:, vs_] = hn.astype(h_ref.dtype)

        m_new = m_t[rb - 1:rb, :]
        bc_last = bcum_c[rb - 1:rb, 0:1]
        w_end = jnp.exp(bc_last - bcum_c + ig_c - m_new)
        dec = jnp.exp(bc_last + m_prev - m_new)
        kw = ks * jnp.concatenate([w_end] * (dk // LANES), axis=1)
        C_new[u] = dec * Cu + _dot_tn(kw.astype(BF16), vs_bf)
        n_new[u] = dec * n_old[u] + jnp.sum(kw, axis=0, keepdims=True)
        m_new_ref[u] = m_new

    if not single:
        @pl.when(c == nc - 1)
        def _():
            Cout_ref[...] = C_scr[...]
            nout_ref[...] = n_scr[...]
            mout_ref[...] = m_scr[...]


def _ml_call(rows, hb, P, side, C_in, n_in, m_in, b_gates, norm_g, y_dtype):
    n_seq, nc, rb = rows.n_seq, rows.n_chunks, rows.rb
    L = nc * rb
    H = ML_HEADS
    dk, dv = C_in.shape[2], C_in.shape[3]
    shared = C_in.shape[0] == 1 and n_seq > 1
    sb = (lambda b: 0) if shared else (lambda b: b)
    assert H % hb == 0

    g4 = side[:, :2 * H].reshape(n_seq, L, 2, H)
    gc = jnp.pad(jnp.transpose(g4, (3, 0, 1, 2)), ((0, 0), (0, 0), (0, 0), (0, LANES - 2)))
    gr = jnp.pad(jnp.transpose(g4, (3, 0, 2, 1)), ((0, 0), (0, 0), (0, 6), (0, 0)))
    bg = jnp.transpose(b_gates.reshape(2, H))
    bgr = jnp.pad(bg, ((0, 0), (0, LANES - 2))).reshape(H, 1, LANES)
    bgc = jnp.pad(bg, ((0, 0), (0, 6))).reshape(H, 8, 1)
    ng = norm_g.reshape(1, H * dv)

    nq, nv = H * dk, H * dv
    wk, wv = hb * dk, hb * dv
    q_c, k_c = (lambda h: h), (lambda h: nq // wk + h)
    v_c, o_c, z_c = (lambda h: 2 * nq // wv + h), (lambda h: (2 * nq + nv) // wv + h), (lambda h: (2 * nq + 2 * nv) // wv + h)
    ins = [P, P, P, P, P, gc, gr, bgr, bgc, C_in, n_in, m_in, ng]
    specs = [rows.spec(wk, q_c), rows.spec(wk, k_c), rows.spec(wv, v_c), rows.spec(wv, o_c), rows.spec(wv, z_c),
             pl.BlockSpec((hb, None, rb, LANES), lambda b, h, c: (h, b, c, 0)),
             pl.BlockSpec((hb, None, 8, rb), lambda b, h, c: (h, b, 0, c)),
             pl.BlockSpec((hb, 1, LANES), lambda b, h, c: (h, 0, 0)),
             pl.BlockSpec((hb, 8, 1), lambda b, h, c: (h, 0, 0)),
             pl.BlockSpec((None, hb, dk, dv), lambda b, h, c: (sb(b), h, 0, 0)),
             pl.BlockSpec((None, hb, 1, dk), lambda b, h, c: (sb(b), h, 0, 0)),
             pl.BlockSpec((None, hb, 1, 1), lambda b, h, c: (sb(b), h, 0, 0)),
             pl.BlockSpec((1, wv), lambda b, h, c: (0, h))]
    out_shape = [rows.out_shape(nv, y_dtype),
                 jax.ShapeDtypeStruct((n_seq, H, dk, dv), F32),
                 jax.ShapeDtypeStruct((n_seq, H, 1, dk), F32),
                 jax.ShapeDtypeStruct((n_seq, H, 1, 1), F32)]
    out_specs = [rows.out_spec(wv),
                 pl.BlockSpec((None, hb, dk, dv), lambda b, h, c: (b, h, 0, 0)),
                 pl.BlockSpec((None, hb, 1, dk), lambda b, h, c: (b, h, 0, 0)),
                 pl.BlockSpec((None, hb, 1, 1), lambda b, h, c: (b, h, 0, 0))]
    single = nc == 1
    ls = _ls(rb)
    scratch = [] if single else [pltpu.VMEM((hb, dk, dv), F32), pltpu.VMEM((hb, 1, dk), F32), pltpu.VMEM((hb, 1, 1), F32)]
    if rb < ls:
        scratch += [pltpu.VMEM((ls, dk), F32), pltpu.VMEM((ls, dv), F32), pltpu.VMEM((ls, dk), F32),
                    pltpu.VMEM((ls, LANES), F32), pltpu.VMEM((ls, LANES), F32), pltpu.VMEM((8, ls), F32)]
    return pl.pallas_call(
        functools.partial(_ml_body, rb=rb, hb=hb, dk=dk, dv=dv, scale=float(dk) ** -0.5, single=single),
        grid=(n_seq, H // hb, nc), in_specs=specs, out_specs=out_specs, out_shape=out_shape, scratch_shapes=scratch,
        compiler_params=_params(("parallel", "parallel", "arbitrary")), name="mlstm")(*ins)


def _ret_body(*refs, rb, hb, dk, dv, scale, single):
    ls = _ls(rb)
    it = iter(refs)
    q_ref, k_ref, v_ref, g_ref, cos_ref, sin_ref, lg_ref, S0_ref, ng_ref = (next(it) for _ in range(9))
    y_ref, Sout_ref = next(it), next(it)
    if single:
        S_old, S_new = S0_ref, Sout_ref
    else:
        S_scr = next(it)
        S_old = S_new = S_scr
    if rb < ls:
        qpad, kpad, vpad = next(it), next(it), next(it)
    else:
        qpad = kpad = vpad = None

    c = pl.program_id(2)
    nc = pl.num_programs(2)

    if not single:
        @pl.when(c == 0)
        def _():
            S_scr[...] = S0_ref[...]

    cos, sin = cos_ref[...], sin_ref[...]
    half = cos.shape[1]

    def rope(x):
        x1, x2 = x[:, :half], x[:, half:]
        return jnp.concatenate([x1 * cos - x2 * sin, x1 * sin + x2 * cos], axis=1)

    row = lax.broadcasted_iota(jnp.int32, (ls, ls), 0)
    col = lax.broadcasted_iota(jnp.int32, (ls, ls), 1)
    causal = row >= col
    rel = (row - col).astype(F32)
    trow = lax.broadcasted_iota(jnp.int32, (ls, LANES), 0)
    nrep = dv // LANES

    for u in range(hb):
        ks_ = slice(u * dk, (u + 1) * dk)
        vs_ = slice(u * dv, (u + 1) * dv)
        q_bf = _pad_rows(rope(q_ref[:, ks_]), qpad).astype(BF16)
        ks = _pad_rows(rope(k_ref[:, ks_]) * scale, kpad)
        vs_bf = _pad_rows(v_ref[:, vs_], vpad).astype(BF16)
        lg = lg_ref[u]
        Su = S_old[u]

        dmat = jnp.exp(jnp.where(causal, rel * lg[:, 0:ls], -jnp.inf))
        a = _dot_nt(q_bf, ks.astype(BF16)) * dmat
        e_t = jnp.exp((trow + 1).astype(F32) * lg)
        y = _dot(a.astype(BF16), vs_bf) + _dot(q_bf, Su.astype(BF16)) * jnp.concatenate([e_t] * nrep, axis=1)

        yr = y[0:rb, :]
        yn = yr * lax.rsqrt(jnp.mean(yr * yr, axis=-1, keepdims=True) + EPS) * ng_ref[:, vs_] * _silu(g_ref[:, vs_])
        y_ref[:, vs_] = yn.astype(y_ref.dtype)

        w_end = jnp.where(trow < rb, jnp.exp((rb - 1 - trow).astype(F32) * lg), 0.0)
        kw = ks * jnp.concatenate([w_end] * (dk // LANES), axis=1)
        e_all = jnp.exp(float(rb) * lg)
        S_new[u] = Su * jnp.concatenate([e_all] * nrep, axis=1) + _dot_tn(kw.astype(BF16), vs_bf)

    if not single:
        @pl.when(c == nc - 1)
        def _():
            Sout_ref[...] = S_scr[...]


def _ret_call(rows, hb, P, S_in, cos, sin, norm_g, y_dtype):
    n_seq, nc, rb = rows.n_seq, rows.n_chunks, rows.rb
    H = RET_HEADS
    dk, dv = S_in.shape[2], S_in.shape[3]
    shared = S_in.shape[0] == 1 and n_seq > 1
    sb = (lambda b: 0) if shared else (lambda b: b)
    assert H % hb == 0
    lg = jnp.log1p(-jnp.exp2(-5.0 - jnp.arange(H, dtype=F32)))
    lg = jnp.broadcast_to(lg[:, None, None], (H, 1, LANES))
    ng = norm_g.reshape(1, H * dv)
    nq, nv = H * dk, H * dv
    wk, wv = hb * dk, hb * dv
    q_c, k_c = (lambda h: h), (lambda h: nq // wk + h)
    v_c, g_c = (lambda h: 2 * nq // wv + h), (lambda h: (2 * nq + nv) // wv + h)
    ins = [P, P, P, P, cos, sin, lg, S_in, ng]
    specs = [rows.spec(wk, q_c), rows.spec(wk, k_c), rows.spec(wv, v_c), rows.spec(wv, g_c),
             pl.BlockSpec((rb, dk // 2), lambda b, h, c: (c, 0)),
             pl.BlockSpec((rb, dk // 2), lambda b, h, c: (c, 0)),
             pl.BlockSpec((hb, 1, LANES), lambda b, h, c: (h, 0, 0)),
             pl.BlockSpec((None, hb, dk, dv), lambda b, h, c: (sb(b), h, 0, 0)),
             pl.BlockSpec((1, wv), lambda b, h, c: (0, h))]
    out_shape = [rows.out_shape(nv, y_dtype), jax.ShapeDtypeStruct((n_seq, H, dk, dv), F32)]
    out_specs = [rows.out_spec(wv), pl.BlockSpec((None, hb, dk, dv), lambda b, h, c: (b, h, 0, 0))]
    single = nc == 1
    ls = _ls(rb)
    scratch = [] if single else [pltpu.VMEM((hb, dk, dv), F32)]
    if rb < ls:
        scratch += [pltpu.VMEM((ls, dk), F32), pltpu.VMEM((ls, dk), F32), pltpu.VMEM((ls, dv), F32)]
    return pl.pallas_call(
        functools.partial(_ret_body, rb=rb, hb=hb, dk=dk, dv=dv, scale=float(dk) ** -0.5, single=single),
        grid=(n_seq, H // hb, nc), in_specs=specs, out_specs=out_specs, out_shape=out_shape, scratch_shapes=scratch,
        compiler_params=_params(("parallel", "parallel", "arbitrary")), name="retention")(*ins)


def _rope_tables(pos, half):
    inv = ROPE_BASE ** (-jnp.arange(half, dtype=F32) / half)
    ang = pos.astype(F32)[:, None] * inv
    return jnp.cos(ang), jnp.sin(ang)


SSD_GB_PROMPT, SSD_GB_SAMPLE = 1, 4
ML_HB_PROMPT, ML_HB_SAMPLE = 2, 8
RET_HB_PROMPT, RET_HB_SAMPLE = 2, 8


def kernel(x_prompt, x_sample, state_ssm, state_ssd_conv, state_mlstm_C, state_mlstm_n, state_mlstm_m, state_ret, meta_tokens, pre_norm_g, post_norm_g, ssd_w_in, ssd_conv_w, ssd_conv_b, ssd_dt_bias, ssd_A_log, ssd_D, ssd_norm_g, ssd_w_out, ml_w_in, ml_b_gates, ml_norm_g, ml_w_out, ret_w_in, ret_norm_g, ret_w_out):
    Bp, Sp, D = x_prompt.shape
    Bd, Ld, _ = x_sample.shape
    n_meta = meta_tokens.shape[0]
    depth = pre_norm_g.shape[0]
    n_p, n_s = Bp * Sp, Bd * Ld
    R = n_p + n_s + n_meta
    assert Sp % CHUNK == 0 and (n_p + n_s) % n_meta == 0 and n_meta % 8 == 0 and Ld <= 8

    rows_p = _Rows("flat", Bp, Sp // CHUNK, CHUNK, 0, y_rows=R)
    rows_s = _Rows("seq", Bd, 1, Ld)
    rows_m = _Rows("flat", 1, 1, n_meta, n_p + n_s)

    def sample_rows(a):
        return a[n_p:n_p + n_s].reshape(Bd, Ld, a.shape[1])

    def join(y, ys, ym):
        y = lax.dynamic_update_slice(y, ys.reshape(n_s, -1).astype(BF16), (n_p, 0))
        return lax.dynamic_update_slice(y, ym, (n_p + n_s, 0))

    x = jnp.concatenate([x_prompt.reshape(n_p, D), x_sample.reshape(n_s, D), meta_tokens.astype(x_prompt.dtype)], axis=0)

    def side_w(w, lo, n):
        return jnp.pad(w[:, lo:lo + n], ((0, 0), (0, LANES - n)))

    n_ssd = state_ssm.shape[0]
    H_ssd, P_ssd, N_ssd = state_ssm.shape[2:]
    d_inner = H_ssd * P_ssd
    conv_dim = state_ssd_conv.shape[-1]
    ssd_main = d_inner + conv_dim
    ml_dk, ml_dv = state_mlstm_C.shape[3:]
    ml_main = 2 * ML_HEADS * ml_dk + 3 * ML_HEADS * ml_dv
    ret_dk, ret_dv = state_ret.shape[3:]
    ret_main = 2 * RET_HEADS * ret_dk + 2 * RET_HEADS * ret_dv
    ssm_in_s = state_ssm.reshape(n_ssd, Bd, d_inner, N_ssd)

    def side_for(i):
        kind, j = i % 3, i // 3
        if kind == 0:
            return side_w(ssd_w_in[j], ssd_main, H_ssd)
        if kind == 1:
            return side_w(ml_w_in[j], ml_main, 2 * ML_HEADS)
        return None

    cos_p, sin_p = _rope_tables(n_meta + jnp.arange(Sp), ret_dk // 2)
    cos_s, sin_s = _rope_tables(PAST_LEN + jnp.arange(Ld), ret_dk // 2)
    cos_m, sin_m = _rope_tables(jnp.arange(n_meta), ret_dk // 2)

    ssm_p, conv_p, mC_p, mn_p, mm_p, ret_p = [], [], [], [], [], []
    conv_s, mC_s, mn_s, mm_s, ret_s = [], [], [], [], []
    ssm_s = None

    sw = side_for(0)
    h, side = _norm_call(x, g_pre=pre_norm_g[0], w_side=sw)
    for i in range(depth):
        kind, j = i % 3, i // 3
        if kind == 0:
            P = _matmul(h, ssd_w_in, j, ssd_main)
            args = (ssd_conv_w[j], ssd_conv_b[j], ssd_dt_bias[j], ssd_A_log[j], ssd_D[j], ssd_norm_g[j])
            zc = jnp.zeros((1, 1, SSD_CONV_W - 1, conv_dim), F32)
            zs = jnp.zeros((1, 1, d_inner, N_ssd), F32)
            ym, cx, cB, cC, sm = _ssd_call(rows_m, SSD_GB_PROMPT, P, side[n_p + n_s:], zc, zs, 0, *args, BF16)
            conv_m = jnp.concatenate([cx, cB, cC], axis=-1)
            yp, cx, cB, cC, sp = _ssd_call(rows_p, SSD_GB_PROMPT, P, side[:n_p], conv_m[None], sm[None], 0, *args, BF16)
            conv_p.append(jnp.concatenate([cx, cB, cC], axis=-1))
            ssm_p.append(sp.reshape(Bp, H_ssd, P_ssd, N_ssd))
            ys, cx, cB, cC, ssm_s = _ssd_call(rows_s, SSD_GB_SAMPLE, sample_rows(P), side[n_p:n_p + n_s],
                                              state_ssd_conv, ssm_in_s, j, *args, F32,
                                              ssm_alias=ssm_s, n_layers_out=n_ssd)
            conv_s.append(jnp.concatenate([cx, cB, cC], axis=-1))
            out = _matmul(join(yp, ys, ym), ssd_w_out, j, D)
        elif kind == 1:
            P = _matmul(h, ml_w_in, j, ml_main)
            args = (ml_b_gates[j], ml_norm_g[j])
            z = lambda *s: jnp.zeros(s, F32)
            ym, Cm, nm, mm = _ml_call(rows_m, ML_HB_PROMPT, P, side[n_p + n_s:], z(1, ML_HEADS, ml_dk, ml_dv),
                                      z(1, ML_HEADS, 1, ml_dk), z(1, ML_HEADS, 1, 1), *args, BF16)
            yp, Cp, np_, mp = _ml_call(rows_p, ML_HB_PROMPT, P, side[:n_p], Cm, nm, mm, *args, BF16)
            mC_p.append(Cp)
            mn_p.append(np_.reshape(Bp, ML_HEADS, ml_dk))
            mm_p.append(mp.reshape(Bp, ML_HEADS))
            ys, Cs, ns_, ms = _ml_call(rows_s, ML_HB_SAMPLE, sample_rows(P), side[n_p:n_p + n_s], state_mlstm_C[j],
                                       state_mlstm_n[j].reshape(Bd, ML_HEADS, 1, ml_dk),
                                       state_mlstm_m[j].reshape(Bd, ML_HEADS, 1, 1), *args, F32)
            mC_s.append(Cs)
            mn_s.append(ns_.reshape(Bd, ML_HEADS, ml_dk))
            mm_s.append(ms.reshape(Bd, ML_HEADS))
            out = _matmul(join(yp, ys, ym), ml_w_out, j, D)
        else:
            P = _matmul(h, ret_w_in, j, ret_main)
            ym, Sm = _ret_call(rows_m, RET_HB_PROMPT, P, jnp.zeros((1, RET_HEADS, ret_dk, ret_dv), F32), cos_m, sin_m,
                               ret_norm_g[j], BF16)
            yp, Sp_ = _ret_call(rows_p, RET_HB_PROMPT, P, Sm, cos_p, sin_p, ret_norm_g[j], BF16)
            ret_p.append(Sp_)
            ys, Ss = _ret_call(rows_s, RET_HB_SAMPLE, sample_rows(P), state_ret[j], cos_s, sin_s, ret_norm_g[j], F32)
            ret_s.append(Ss)
            out = _matmul(join(yp, ys, ym), ret_w_out, j, D)
        if i + 1 < depth:
            sw = side_for(i + 1)
            res = _norm_call(x, o=out, g_post=post_norm_g[i], g_pre=pre_norm_g[i + 1], w_side=sw)
            x, h = res[0], res[1]
            side = res[2] if sw is not None else None
        else:
            (x,) = _norm_call(x, o=out, g_post=post_norm_g[i])

    y_prompt = x[:n_p].reshape(Bp, Sp, D)
    y_sample = x[n_p:n_p + n_s].reshape(Bd, Ld, D)
    return (y_prompt, y_sample, _stack(ssm_p), _stack(conv_p), _stack(mC_p), _stack(mn_p), _stack(mm_p), _stack(ret_p),
            ssm_s.reshape(n_ssd, Bd, H_ssd, P_ssd, N_ssd), _stack(conv_s), _stack(mC_s), _stack(mn_s), _stack(mm_s),
            _stack(ret_s))
```

```python
import functools
import math

import jax
import jax.numpy as jnp
from jax import lax
from jax.experimental import pallas as pl
from jax.experimental.pallas import tpu as pltpu

F32 = jnp.float32
BF16 = jnp.bfloat16

EPS = 1e-6
CHUNK = 128
LANES = 128
SUBLANES = 8
PAST_LEN = 16384
ROPE_BASE = 10000.0
K_BLOCK = 4096
NEG_BIG = -1e30
LOG2E = 1.4426950408889634
VMEM_LIMIT = 56 * 1024 * 1024

SSD_GROUPS = 8
SSD_HEAD_DIM = 64
SSD_CONV_W = 4
ML_HEADS = 8
RET_HEADS = 16


def _dot(a, b):
    return jnp.dot(a, b, preferred_element_type=F32)


def _dot_nt(a, b):
    return lax.dot_general(a, b, (((1,), (1,)), ((), ())), preferred_element_type=F32)


def _dot_tn(a, b):
    return lax.dot_general(a, b, (((0,), (0,)), ((), ())), preferred_element_type=F32)


def _split3(x):
    hi = x.astype(BF16)
    r1 = x - hi.astype(F32)
    mid = r1.astype(BF16)
    lo = (r1 - mid.astype(F32)).astype(BF16)
    return hi, mid, lo


def _cum_left(tri, x):
    hi, mid, lo = _split3(x)
    return _dot(tri, hi) + _dot(tri, mid) + _dot(tri, lo)


def _cum_right(x, tri):
    hi, mid, lo = _split3(x)
    return _dot(hi, tri) + _dot(mid, tri) + _dot(lo, tri)


def _tri(n, lower):
    r = lax.broadcasted_iota(jnp.int32, (n, n), 0)
    c = lax.broadcasted_iota(jnp.int32, (n, n), 1)
    m = (r >= c) if lower else (r <= c)
    return jnp.where(m, 1.0, 0.0).astype(BF16)


def _softplus(x):
    return jnp.maximum(x, 0.0) + jnp.log1p(jnp.exp(-jnp.abs(x)))


def _log_sigmoid(x):
    return jnp.minimum(x, 0.0) - jnp.log1p(jnp.exp(-jnp.abs(x)))


def _sigmoid(x):
    return 0.5 + 0.5 * jnp.tanh(0.5 * x)


def _silu(x):
    h = 0.5 * x
    return h + h * jnp.tanh(h)


def _pad_rows(val, pad_ref, fill=0.0):
    if pad_ref is None:
        return val
    sr = val.shape[0]
    pad_ref[...] = jnp.full(pad_ref.shape, fill, pad_ref.dtype)
    pad_ref[0:sr, :] = val
    return pad_ref[...]


def _pad_lanes(val, pad_ref, fill=0.0):
    if pad_ref is None:
        return val
    sr = val.shape[1]
    pad_ref[...] = jnp.full(pad_ref.shape, fill, pad_ref.dtype)
    pad_ref[:, 0:sr] = val
    return pad_ref[...]


def _row_tile(n, cap):
    best = None
    for t in range(16, min(n, cap) + 1, 16):
        if n % t == 0:
            best = t
    return best if best is not None else n


def _params(sem, limit=VMEM_LIMIT):
    return pltpu.CompilerParams(dimension_semantics=sem, vmem_limit_bytes=limit)


def _stack(parts):
    return parts[0][None] if len(parts) == 1 else jnp.stack(parts)


def _norm_body(*refs, has_post, has_pre, has_side):
    refs = list(refs)
    if has_post:
        o_ref, x_ref, gpost_ref = refs[:3]
        refs = refs[3:]
    else:
        x_ref = refs[0]
        refs = refs[1:]
    if has_pre:
        gpre_ref = refs[0]
        refs = refs[1:]
    if has_side:
        ws_ref = refs[0]
        refs = refs[1:]
    outs = refs
    x = x_ref[...]
    if has_post:
        o = o_ref[...]
        x = x + o * lax.rsqrt(jnp.mean(o * o, axis=-1, keepdims=True) + EPS) * gpost_ref[...]
        outs[0][...] = x
        outs = outs[1:]
    if has_pre:
        h = (x * lax.rsqrt(jnp.mean(x * x, axis=-1, keepdims=True) + EPS) * gpre_ref[...]).astype(BF16)
        outs[0][...] = h
        if has_side:
            outs[1][...] = _dot(h, ws_ref[...].astype(BF16))


def _norm_call(x, o=None, g_post=None, g_pre=None, w_side=None, row_lo=0, n_rows=None):
    D = x.shape[1]
    R = x.shape[0] if n_rows is None else n_rows
    has_post, has_pre, has_side = o is not None, g_pre is not None, w_side is not None
    tr = _row_tile(math.gcd(R, row_lo) if row_lo else R, 256)
    assert R % tr == 0 and row_lo % tr == 0
    row_in = pl.BlockSpec((tr, D), lambda i: (row_lo // tr + i, 0))
    row = pl.BlockSpec((tr, D), lambda i: (i, 0))
    vec = pl.BlockSpec((1, D), lambda i: (0, 0))
    ins, specs = [], []
    if has_post:
        ins += [o, x, g_post.reshape(1, D)]
        specs += [row_in, row_in, vec]
    else:
        ins += [x]
        specs += [row_in]
    if has_pre:
        ins += [g_pre.reshape(1, D)]
        specs += [vec]
    if has_side:
        ins += [w_side]
        specs += [pl.BlockSpec(w_side.shape, lambda i: (0, 0))]
    out_shape, out_specs = [], []
    if has_post:
        out_shape.append(jax.ShapeDtypeStruct((R, D), F32))
        out_specs.append(row)
    if has_pre:
        out_shape.append(jax.ShapeDtypeStruct((R, D), BF16))
        out_specs.append(row)
        if has_side:
            out_shape.append(jax.ShapeDtypeStruct((R, w_side.shape[1]), F32))
            out_specs.append(pl.BlockSpec((tr, w_side.shape[1]), lambda i: (i, 0)))
    return pl.pallas_call(
        functools.partial(_norm_body, has_post=has_post, has_pre=has_pre, has_side=has_side),
        grid=(R // tr,), in_specs=specs, out_specs=out_specs, out_shape=out_shape,
        compiler_params=_params(("parallel",)), name="norm")(*ins)


def _mm_body(x_ref, w_ref, o_ref):
    o_ref[...] = _dot(x_ref[...], w_ref[...].astype(BF16))


def _mm_acc_body(x_ref, w_ref, a_ref, o_ref):
    o_ref[...] = a_ref[...] + _dot(x_ref[...], w_ref[...].astype(BF16))


def _matmul(x, w, layer, n_cols):
    R, K = x.shape
    tm = _row_tile(R, 2048)
    tn = 512
    out = None
    for kb in range(K // K_BLOCK):
        x_spec = pl.BlockSpec((tm, K_BLOCK), lambda i, j, kb=kb: (i, kb), pipeline_mode=pl.Buffered(1))
        w_spec = pl.BlockSpec((None, K_BLOCK, tn), lambda i, j, kb=kb: (layer, kb, j))
        o_spec = pl.BlockSpec((tm, tn), lambda i, j: (i, j))
        shape = jax.ShapeDtypeStruct((R, n_cols), F32)
        if out is None:
            out = pl.pallas_call(_mm_body, grid=(R // tm, n_cols // tn), in_specs=[x_spec, w_spec],
                                 out_specs=o_spec, out_shape=shape,
                                 compiler_params=_params(("parallel", "arbitrary")), name="proj")(x, w)
        else:
            out = pl.pallas_call(_mm_acc_body, grid=(R // tm, n_cols // tn), in_specs=[x_spec, w_spec, o_spec],
                                 out_specs=o_spec, out_shape=shape, input_output_aliases={2: 0},
                                 compiler_params=_params(("parallel", "arbitrary")), name="proj_acc")(x, w, out)
    return out


class _Rows:
    def __init__(self, n_blk, bb, sr, n_chunks, row_off=0, y_rows=None):
        self.n_blk, self.bb, self.sr, self.n_chunks, self.row_off, self.y_rows = n_blk, bb, sr, n_chunks, row_off, y_rows
        self.rb = bb * sr
        self.n_seq = n_blk * bb
        assert row_off % self.rb == 0 and (bb == 1 or n_chunks == 1)

    def spec(self, width, col_fn):
        rb, nc, off = self.rb, self.n_chunks, self.row_off // self.rb
        return pl.BlockSpec((rb, width), lambda b, h, c: (off + b * nc + c, col_fn(h)))

    def prev_spec(self, width, col_fn):
        nc, k = self.n_chunks, self.rb // SUBLANES
        return pl.BlockSpec((SUBLANES, width), lambda b, h, c: (jnp.maximum((b * nc + c) * k - 1, 0), col_fn(h)))

    def out_shape(self, width, dtype):
        rows = self.y_rows if self.y_rows is not None else self.n_blk * self.n_chunks * self.rb
        return jax.ShapeDtypeStruct((rows, width), dtype)

    def out_spec(self, width):
        rb, nc = self.rb, self.n_chunks
        return pl.BlockSpec((rb, width), lambda b, h, c: (b * nc + c, h))


def _ls(sr):
    return max(SUBLANES, sr)


def _ssd_body(*refs, bb, sr, has_prev, gb, gw, has_alias):
    ls = _ls(sr)
    single = not has_prev
    it = iter(refs)
    z_ref, x_ref, B_ref, C_ref = next(it), next(it), next(it), next(it)
    if has_prev:
        xp_ref, Bp_ref, Cp_ref = next(it), next(it), next(it)
    else:
        xp_ref = Bp_ref = Cp_ref = None
    csx_ref, csB_ref, csC_ref = next(it), next(it), next(it)
    cwx_ref, cwB_ref, cwC_ref = next(it), next(it), next(it)
    cbx_ref, cbB_ref, cbC_ref = next(it), next(it), next(it)
    dtc_ref, dtr_ref, hpr_ref, hpc_ref, D_ref, ng_ref, s0_ref = (next(it) for _ in range(7))
    if has_alias:
        next(it)
    y_ref, ocx_ref, ocB_ref, ocC_ref, sout_ref = (next(it) for _ in range(5))
    S_scr = None if single else next(it)
    ex_all, eB_all, eC_all, xw_all, y_all = (next(it) for _ in range(5))
    pads_all = [next(it) for _ in range(5)] if sr < ls else None

    c = pl.program_id(2)
    nc = pl.num_programs(2)

    if not single:
        @pl.when(c == 0)
        def _():
            S_scr[...] = s0_ref[...]

    row = lax.broadcasted_iota(jnp.int32, (ls, ls), 0)
    col = lax.broadcasted_iota(jnp.int32, (ls, ls), 1)
    causal = row >= col
    lo_s = lax.broadcasted_iota(jnp.int32, (ls, LANES), 1) < SSD_HEAD_DIM
    tri_l, tri_u = _tri(ls, True), _tri(ls, False)
    hg = gw // SSD_HEAD_DIM

    for i in range(bb):
        rows = slice(i * sr, (i + 1) * sr)
        S_old = s0_ref.at[i] if single else S_scr.at[i]
        S_new = sout_ref.at[i] if single else S_scr.at[i]

        def conv(cur_ref, prev_ref, cs_ref, w_ref, b_ref, ext, oc_ref):
            if has_prev:
                prev3 = jnp.where(c == 0, cs_ref[i], prev_ref[5:8, :])
            else:
                prev3 = cs_ref[i]
            ext[5:8, :] = prev3
            ext[8:8 + sr, :] = cur_ref[rows, :]
            if sr % SUBLANES == 0:
                v = ext[...]
                acc = b_ref[...] + v[8:8 + sr, :] * w_ref[3:4, :]
                for k in range(SSD_CONV_W - 1):
                    acc = acc + pltpu.roll(v, SSD_CONV_W - 1 - k, axis=0)[8:8 + sr, :] * w_ref[k:k + 1, :]
            else:
                acc = b_ref[...] + ext[5:5 + sr, :] * w_ref[0:1, :]
                for k in range(1, SSD_CONV_W):
                    acc = acc + ext[5 + k:5 + k + sr, :] * w_ref[k:k + 1, :]
            oc_ref[i] = ext[5 + sr:8 + sr, :]
            return _silu(acc)

        x_all = conv(x_ref, xp_ref, csx_ref, cwx_ref, cbx_ref, ex_all.at[i], ocx_ref)
        B_all = conv(B_ref, Bp_ref, csB_ref, cwB_ref, cbB_ref, eB_all.at[i], ocB_ref)
        C_all = conv(C_ref, Cp_ref, csC_ref, cwC_ref, cbC_ref, eC_all.at[i], ocC_ref)

        for u in range(gb):
            gs = slice(u * gw, (u + 1) * gw)
            ns = slice(u * LANES, (u + 1) * LANES)
            slot = i * gb + u
            xw_scr, y_scr = xw_all.at[slot], y_all.at[slot]
            if pads_all is None:
                xpad = Bpad = Cpad = dtcpad = dtrpad = None
            else:
                xpad, Bpad, Cpad, dtcpad, dtrpad = (p.at[slot] for p in pads_all)
            xs = _pad_rows(x_all[:, gs], xpad)
            Bs = _pad_rows(B_all[:, ns], Bpad)
            Cs = _pad_rows(C_all[:, ns], Cpad)

            hpr = hpr_ref[u]
            hpc = hpc_ref[u]
            dtc = _pad_rows(_softplus(dtc_ref[u, i] + hpr[0:1, :]), dtcpad)
            dtr = _pad_lanes(_softplus(dtr_ref[u, i] + hpc[:, 0:1]), dtrpad)
            a_c = dtc * (-LOG2E * jnp.exp(hpr[1:2, :]))
            a_r = dtr * (-LOG2E * jnp.exp(hpc[:, 1:2]))
            cum_c = _cum_left(tri_l, a_c)
            cum_r = _cum_right(a_r, tri_u)
            tot_r = jnp.sum(a_r, axis=1, keepdims=True)
            tot_c = cum_c[ls - 1:ls, :]
            e_c = jnp.exp2(cum_c)
            te_c = jnp.exp2(tot_c - cum_c) * dtc
            r2 = cum_r - jnp.log2(dtr)

            Bs_bf = Bs.astype(BF16)
            Cs_bf = Cs.astype(BF16)
            cb = _dot_nt(Cs_bf, Bs_bf)
            Yi = _dot_nt(Cs_bf, S_old[gs, :].astype(BF16))

            for jj in range(hg // 2):
                j0, j1 = 2 * jj, 2 * jj + 1
                sl = slice(LANES * jj, LANES * (jj + 1))
                ws = []
                for j in (j0, j1):
                    seg = cum_c[:, j:j + 1] - r2[j:j + 1, :]
                    ws.append((jnp.exp2(jnp.where(causal, seg, -jnp.inf)) * cb).astype(BF16))
                xp = xs[:, sl]
                xp_bf = xp.astype(BF16)
                if ls == CHUNK:
                    both = _dot(jnp.concatenate(ws, axis=0), xp_bf)
                    intra = jnp.where(lo_s, both[0:ls, :], both[ls:2 * ls, :])
                else:
                    intra = jnp.where(lo_s, _dot(ws[0], xp_bf), _dot(ws[1], xp_bf))
                e_pair = jnp.where(lo_s, e_c[:, j0:j0 + 1], e_c[:, j1:j1 + 1])
                dsl = slice(u * gw + LANES * jj, u * gw + LANES * (jj + 1))
                y_scr[:, sl] = intra + Yi[:, sl] * e_pair + xp * D_ref[:, dsl]
                te_pair = jnp.where(lo_s, te_c[:, j0:j0 + 1], te_c[:, j1:j1 + 1])
                xw_scr[:, sl] = (xp * te_pair).astype(xw_scr.dtype)

            yg = y_scr[0:sr, :] * _silu(z_ref[rows, gs])
            yn = yg * lax.rsqrt(jnp.mean(yg * yg, axis=-1, keepdims=True) + EPS) * ng_ref[:, gs]
            y_ref[rows, gs] = yn.astype(y_ref.dtype)

            upd = _dot_tn(xw_scr[...].astype(BF16), Bs_bf)
            etot = jnp.broadcast_to(jnp.exp2(tot_r), (hg, LANES))
            for j in range(hg):
                rs = slice(u * gw + SSD_HEAD_DIM * j, u * gw + SSD_HEAD_DIM * (j + 1))
                us = slice(SSD_HEAD_DIM * j, SSD_HEAD_DIM * (j + 1))
                S_new[rs, :] = S_old[rs, :] * etot[j:j + 1, :] + upd[us, :]

    if not single:
        @pl.when(c == nc - 1)
        def _():
            sout_ref[...] = S_scr[...]


def _ssd_call(rows, gb, P, side, conv_in, ssm_in, layer, conv_w, conv_b, dt_bias, A_log, d_skip, norm_g, y_dtype,
              ssm_alias=None, n_layers_out=None):
    n_seq, nc, bb, sr = rows.n_seq, rows.n_chunks, rows.bb, rows.sr
    L = nc * sr
    G = SSD_GROUPS
    d_inner = d_skip.shape[0] * SSD_HEAD_DIM
    gw = d_inner // G
    n_state = ssm_in.shape[-1]
    hg = gw // SSD_HEAD_DIM
    shared = conv_in.shape[1] == 1 and n_seq > 1
    sb = (lambda b: 0) if shared else (lambda b: b)
    assert G % gb == 0 and n_state == LANES and not (shared and bb > 1)

    s4 = side.reshape(n_seq, L, G, hg)
    dtc = jnp.pad(jnp.transpose(s4, (2, 0, 1, 3)), ((0, 0), (0, 0), (0, 0), (0, LANES - hg)))
    dtr = jnp.transpose(s4, (2, 0, 3, 1))
    hp = jnp.stack([dt_bias.reshape(G, hg), A_log.reshape(G, hg)], axis=1)
    hpr = jnp.pad(hp, ((0, 0), (0, 0), (0, LANES - hg)))
    hpc = jnp.transpose(hp, (0, 2, 1))
    d_exp = jnp.repeat(d_skip, SSD_HEAD_DIM).reshape(1, d_inner)
    ng = norm_g.reshape(1, d_inner)
    cb2 = conv_b.reshape(1, -1)

    wx, wn = gb * gw, gb * LANES
    xo, Bo, Co = d_inner // wx, 2 * d_inner // wn, (2 * d_inner + G * n_state) // wn
    z_c, x_c = (lambda g: g), (lambda g: xo + g)
    B_c, C_c = (lambda g: Bo + g), (lambda g: Co + g)
    cxo, cBo, cCo = (lambda g: g), (lambda g: d_inner // wn + g), (lambda g: (d_inner + G * n_state) // wn + g)
    has_prev = nc > 1

    ins = [P, P, P, P]
    specs = [rows.spec(wx, z_c), rows.spec(wx, x_c), rows.spec(wn, B_c), rows.spec(wn, C_c)]
    if has_prev:
        ins += [P, P, P]
        specs += [rows.prev_spec(wx, x_c), rows.prev_spec(wn, B_c), rows.prev_spec(wn, C_c)]
    ins += [conv_in, conv_in, conv_in, conv_w, conv_w, conv_w, cb2, cb2, cb2]
    specs += [pl.BlockSpec((None, bb, 3, wx), lambda b, g, c: (layer, sb(b), 0, cxo(g))),
              pl.BlockSpec((None, bb, 3, wn), lambda b, g, c: (layer, sb(b), 0, cBo(g))),
              pl.BlockSpec((None, bb, 3, wn), lambda b, g, c: (layer, sb(b), 0, cCo(g))),
              pl.BlockSpec((SSD_CONV_W, wx), lambda b, g, c: (0, cxo(g))),
              pl.BlockSpec((SSD_CONV_W, wn), lambda b, g, c: (0, cBo(g))),
              pl.BlockSpec((SSD_CONV_W, wn), lambda b, g, c: (0, cCo(g))),
              pl.BlockSpec((1, wx), lambda b, g, c: (0, cxo(g))),
              pl.BlockSpec((1, wn), lambda b, g, c: (0, cBo(g))),
              pl.BlockSpec((1, wn), lambda b, g, c: (0, cCo(g)))]
    ins += [dtc, dtr, hpr, hpc, d_exp, ng, ssm_in]
    specs += [pl.BlockSpec((gb, bb, sr, LANES), lambda b, g, c: (g, b, c, 0)),
              pl.BlockSpec((gb, bb, hg, sr), lambda b, g, c: (g, b, 0, c)),
              pl.BlockSpec((gb, 2, LANES), lambda b, g, c: (g, 0, 0)),
              pl.BlockSpec((gb, hg, 2), lambda b, g, c: (g, 0, 0)),
              pl.BlockSpec((1, wx), lambda b, g, c: (0, g)),
              pl.BlockSpec((1, wx), lambda b, g, c: (0, g)),
              pl.BlockSpec((None, bb, wx, n_state), lambda b, g, c: (layer, sb(b), g, 0))]
    aliases = {}
    if ssm_alias is not None:
        aliases = {len(ins): 4}
        ins += [ssm_alias]
        specs += [pl.BlockSpec(memory_space=pl.ANY)]
    if n_layers_out is None:
        s_shape = jax.ShapeDtypeStruct((n_seq, d_inner, n_state), F32)
        s_spec = pl.BlockSpec((bb, wx, n_state), lambda b, g, c: (b, g, 0))
    else:
        s_shape = jax.ShapeDtypeStruct((n_layers_out, n_seq, d_inner, n_state), F32)
        s_spec = pl.BlockSpec((None, bb, wx, n_state), lambda b, g, c: (layer, b, g, 0))
    out_shape = [rows.out_shape(d_inner, y_dtype),
                 jax.ShapeDtypeStruct((n_seq, 3, d_inner), F32),
                 jax.ShapeDtypeStruct((n_seq, 3, G * n_state), F32),
                 jax.ShapeDtypeStruct((n_seq, 3, G * n_state), F32),
                 s_shape]
    out_specs = [rows.out_spec(wx),
                 pl.BlockSpec((bb, 3, wx), lambda b, g, c: (b, 0, g)),
                 pl.BlockSpec((bb, 3, wn), lambda b, g, c: (b, 0, g)),
                 pl.BlockSpec((bb, 3, wn), lambda b, g, c: (b, 0, g)),
                 s_spec]
    ls = _ls(sr)
    scratch = [] if not has_prev else [pltpu.VMEM((bb, wx, n_state), F32)]
    n_it = bb * gb
    scratch += [pltpu.VMEM((bb, 8 + sr, wx), F32), pltpu.VMEM((bb, 8 + sr, wn), F32), pltpu.VMEM((bb, 8 + sr, wn), F32),
                pltpu.VMEM((n_it, ls, gw), BF16 if ls == CHUNK else F32), pltpu.VMEM((n_it, ls, gw), F32)]
    if sr < ls:
        scratch += [pltpu.VMEM((n_it, ls, gw), F32), pltpu.VMEM((n_it, ls, LANES), F32),
                    pltpu.VMEM((n_it, ls, LANES), F32), pltpu.VMEM((n_it, ls, LANES), F32),
                    pltpu.VMEM((n_it, hg, ls), F32)]
    return pl.pallas_call(
        functools.partial(_ssd_body, bb=bb, sr=sr, has_prev=has_prev, gb=gb, gw=gw, has_alias=ssm_alias is not None),
        grid=(rows.n_blk, G // gb, nc), in_specs=specs, out_specs=out_specs, out_shape=out_shape,
        scratch_shapes=scratch, input_output_aliases=aliases,
        compiler_params=_params(("parallel", "parallel", "arbitrary")), name="ssd")(*ins)


def _ml_body(*refs, bb, sr, hb, dk, dv, scale, single):
    ls = _ls(sr)
    it = iter(refs)
    q_ref, k_ref, v_ref, o_ref, z_ref, gc_ref, gr_ref, bgr_ref, bgc_ref, C0_ref, n0_ref, m0_ref, ng_ref = (
        next(it) for _ in range(13))
    h_ref, Cout_ref, nout_ref, mout_ref = (next(it) for _ in range(4))
    if single:
        C_old, n_old, m_old, C_new, n_new, m_new_ref = C0_ref, n0_ref, m0_ref, Cout_ref, nout_ref, mout_ref
    else:
        C_scr, n_scr, m_scr = next(it), next(it), next(it)
        C_old, n_old, m_old, C_new, n_new, m_new_ref = C_scr, n_scr, m_scr, C_scr, n_scr, m_scr
    pads_all = [next(it) for _ in range(6)] if sr < ls else None

    c = pl.program_id(2)
    nc = pl.num_programs(2)

    if not single:
        @pl.when(c == 0)
        def _():
            C_scr[...] = C0_ref[...]
            n_scr[...] = n0_ref[...]
            m_scr[...] = m0_ref[...]

    row = lax.broadcasted_iota(jnp.int32, (ls, ls), 0)
    col = lax.broadcasted_iota(jnp.int32, (ls, ls), 1)
    causal = row >= col
    tri_l, tri_u = _tri(ls, True), _tri(ls, False)

    for i in range(bb):
        rows = slice(i * sr, (i + 1) * sr)
        for u in range(hb):
            ks_ = slice(u * dk, (u + 1) * dk)
            vs_ = slice(u * dv, (u + 1) * dv)
            if pads_all is None:
                kpad = vpad = qpad = igpad = lfpad = grpad = None
            else:
                kpad, vpad, qpad, igpad, lfpad, grpad = (p.at[i * hb + u] for p in pads_all)
            gc = gc_ref[u, i] + bgr_ref[u]
            ig_c = _pad_rows(jnp.broadcast_to(gc[:, 0:1], (sr, LANES)), igpad, NEG_BIG)
            lf_c = _pad_rows(jnp.broadcast_to(_log_sigmoid(gc[:, 1:2]), (sr, LANES)), lfpad)
            gr = gr_ref[u, i] + bgc_ref[u]
            rsel = lax.broadcasted_iota(jnp.int32, gr.shape, 0)
            gr = jnp.where(rsel == 1, _log_sigmoid(gr), gr)
            if grpad is not None:
                grpad[...] = jnp.zeros(grpad.shape, F32)
                grpad[0:1, :] = jnp.full((1, ls), NEG_BIG, F32)
                grpad[:, 0:sr] = gr
                gr = grpad[...]
            ig_r = gr[0:1, :]
            bcum_c = _cum_left(tri_l, lf_c)
            bcum_r = _cum_right(gr, tri_u)[1:2, :]

            logd = jnp.where(causal, bcum_c[:, 0:ls] - bcum_r + ig_r, -jnp.inf)
            m_prev = m_old[i, u]
            m_inter = bcum_c[:, 0:1] + m_prev
            m_t = jnp.maximum(m_inter, jnp.max(logd, axis=1, keepdims=True))
            dmat = jnp.exp(logd - m_t)
            w_prev = jnp.exp(m_inter - m_t)

            q = _pad_rows(q_ref[rows, ks_], qpad)
            ks = _pad_rows(k_ref[rows, ks_], kpad) * scale
            vs_bf = _pad_rows(v_ref[rows, vs_], vpad).astype(BF16)
            q_bf = q.astype(BF16)
            Cu = C_old[i, u]
            nu = n_old[i, u]
            s = _dot_nt(q_bf, ks.astype(BF16)) * dmat
            num = _dot(s.astype(BF16), vs_bf) + w_prev * _dot(q_bf, Cu.astype(BF16))
            den = jnp.sum(s, axis=1, keepdims=True) + w_prev * jnp.sum(q * nu, axis=1, keepdims=True)
            hout = num / jnp.maximum(jnp.abs(den), jnp.exp(-m_t))

            hc = hout[0:sr, :] * _sigmoid(o_ref[rows, vs_])
            hn = (hc * lax.rsqrt(jnp.mean(hc * hc, axis=-1, keepdims=True) + EPS) * ng_ref[:, vs_]
                  * _silu(z_ref[rows, vs_]))
            h_ref[rows, vs_] = hn.astype(h_ref.dtype)

            m_new = m_t[sr - 1:sr, :]
            bc_last = bcum_c[sr - 1:sr, 0:1]
            w_end = jnp.exp(bc_last - bcum_c + ig_c - m_new)
            dec = jnp.exp(bc_last + m_prev - m_new)
            kw = ks * jnp.concatenate([w_end] * (dk // LANES), axis=1)
            C_new[i, u] = dec * Cu + _dot_tn(kw.astype(BF16), vs_bf)
            n_new[i, u] = dec * nu + jnp.sum(kw, axis=0, keepdims=True)
            m_new_ref[i, u] = m_new

    if not single:
        @pl.when(c == nc - 1)
        def _():
            Cout_ref[...] = C_scr[...]
            nout_ref[...] = n_scr[...]
            mout_ref[...] = m_scr[...]


def _ml_call(rows, hb, P, side, C_in, n_in, m_in, b_gates, norm_g, y_dtype):
    n_seq, nc, bb, sr = rows.n_seq, rows.n_chunks, rows.bb, rows.sr
    L = nc * sr
    H = ML_HEADS
    dk, dv = C_in.shape[2], C_in.shape[3]
    shared = C_in.shape[0] == 1 and n_seq > 1
    sb = (lambda b: 0) if shared else (lambda b: b)
    assert H % hb == 0 and not (shared and bb > 1)

    g4 = side[:, :2 * H].reshape(n_seq, L, 2, H)
    gc = jnp.pad(jnp.transpose(g4, (3, 0, 1, 2)), ((0, 0), (0, 0), (0, 0), (0, LANES - 2)))
    gr = jnp.pad(jnp.transpose(g4, (3, 0, 2, 1)), ((0, 0), (0, 0), (0, 6), (0, 0)))
    bg = jnp.transpose(b_gates.reshape(2, H))
    bgr = jnp.pad(bg, ((0, 0), (0, LANES - 2))).reshape(H, 1, LANES)
    bgc = jnp.pad(bg, ((0, 0), (0, 6))).reshape(H, 8, 1)
    ng = norm_g.reshape(1, H * dv)

    nq, nv = H * dk, H * dv
    wk, wv = hb * dk, hb * dv
    q_c, k_c = (lambda h: h), (lambda h: nq // wk + h)
    v_c, o_c, z_c = (lambda h: 2 * nq // wv + h), (lambda h: (2 * nq + nv) // wv + h), (lambda h: (2 * nq + 2 * nv) // wv + h)
    ins = [P, P, P, P, P, gc, gr, bgr, bgc, C_in, n_in, m_in, ng]
    specs = [rows.spec(wk, q_c), rows.spec(wk, k_c), rows.spec(wv, v_c), rows.spec(wv, o_c), rows.spec(wv, z_c),
             pl.BlockSpec((hb, bb, sr, LANES), lambda b, h, c: (h, b, c, 0)),
             pl.BlockSpec((hb, bb, 8, sr), lambda b, h, c: (h, b, 0, c)),
             pl.BlockSpec((hb, 1, LANES), lambda b, h, c: (h, 0, 0)),
             pl.BlockSpec((hb, 8, 1), lambda b, h, c: (h, 0, 0)),
             pl.BlockSpec((bb, hb, dk, dv), lambda b, h, c: (sb(b), h, 0, 0)),
             pl.BlockSpec((bb, hb, 1, dk), lambda b, h, c: (sb(b), h, 0, 0)),
             pl.BlockSpec((bb, hb, 1, 1), lambda b, h, c: (sb(b), h, 0, 0)),
             pl.BlockSpec((1, wv), lambda b, h, c: (0, h))]
    out_shape = [rows.out_shape(nv, y_dtype),
                 jax.ShapeDtypeStruct((n_seq, H, dk, dv), F32),
                 jax.ShapeDtypeStruct((n_seq, H, 1, dk), F32),
                 jax.ShapeDtypeStruct((n_seq, H, 1, 1), F32)]
    out_specs = [rows.out_spec(wv),
                 pl.BlockSpec((bb, hb, dk, dv), lambda b, h, c: (b, h, 0, 0)),
                 pl.BlockSpec((bb, hb, 1, dk), lambda b, h, c: (b, h, 0, 0)),
                 pl.BlockSpec((bb, hb, 1, 1), lambda b, h, c: (b, h, 0, 0))]
    single = nc == 1
    ls = _ls(sr)
    scratch = [] if single else [pltpu.VMEM((bb, hb, dk, dv), F32), pltpu.VMEM((bb, hb, 1, dk), F32),
                                 pltpu.VMEM((bb, hb, 1, 1), F32)]
    if sr < ls:
        n_it = bb * hb
        scratch += [pltpu.VMEM((n_it, ls, dk), F32), pltpu.VMEM((n_it, ls, dv), F32), pltpu.VMEM((n_it, ls, dk), F32),
                    pltpu.VMEM((n_it, ls, LANES), F32), pltpu.VMEM((n_it, ls, LANES), F32),
                    pltpu.VMEM((n_it, 8, ls), F32)]
    return pl.pallas_call(
        functools.partial(_ml_body, bb=bb, sr=sr, hb=hb, dk=dk, dv=dv, scale=float(dk) ** -0.5, single=single),
        grid=(rows.n_blk, H // hb, nc), in_specs=specs, out_specs=out_specs, out_shape=out_shape,
        scratch_shapes=scratch,
        compiler_params=_params(("parallel", "parallel", "arbitrary")), name="mlstm")(*ins)


def _ret_body(*refs, bb, sr, hb, dk, dv, scale, single):
    ls = _ls(sr)
    it = iter(refs)
    q_ref, k_ref, v_ref, g_ref, cos_ref, sin_ref, lg_ref, S0_ref, ng_ref = (next(it) for _ in range(9))
    y_ref, Sout_ref = next(it), next(it)
    if single:
        S_old, S_new = S0_ref, Sout_ref
    else:
        S_scr = next(it)
        S_old = S_new = S_scr
    pads_all = [next(it) for _ in range(3)] if sr < ls else None

    c = pl.program_id(2)
    nc = pl.num_programs(2)

    if not single:
        @pl.when(c == 0)
        def _():
            S_scr[...] = S0_ref[...]

    cos, sin = cos_ref[...], sin_ref[...]
    half = cos.shape[1]

    def rope(x):
        x1, x2 = x[:, :half], x[:, half:]
        return jnp.concatenate([x1 * cos - x2 * sin, x1 * sin + x2 * cos], axis=1)

    row = lax.broadcasted_iota(jnp.int32, (ls, ls), 0)
    col = lax.broadcasted_iota(jnp.int32, (ls, ls), 1)
    causal = row >= col
    rel = (row - col).astype(F32)
    trow = lax.broadcasted_iota(jnp.int32, (ls, LANES), 0)
    nrep = dv // LANES

    for i in range(bb):
        rows = slice(i * sr, (i + 1) * sr)
        for u in range(hb):
            ks_ = slice(u * dk, (u + 1) * dk)
            vs_ = slice(u * dv, (u + 1) * dv)
            if pads_all is None:
                qpad = kpad = vpad = None
            else:
                qpad, kpad, vpad = (p.at[i * hb + u] for p in pads_all)
            q_bf = _pad_rows(rope(q_ref[rows, ks_]), qpad).astype(BF16)
            ks = _pad_rows(rope(k_ref[rows, ks_]) * scale, kpad)
            vs_bf = _pad_rows(v_ref[rows, vs_], vpad).astype(BF16)
            lg = lg_ref[u]
            Su = S_old[i, u]

            dmat = jnp.exp(jnp.where(causal, rel * lg[:, 0:ls], -jnp.inf))
            a = _dot_nt(q_bf, ks.astype(BF16)) * dmat
            e_t = jnp.exp((trow + 1).astype(F32) * lg)
            y = _dot(a.astype(BF16), vs_bf) + _dot(q_bf, Su.astype(BF16)) * jnp.concatenate([e_t] * nrep, axis=1)

            yr = y[0:sr, :]
            yn = (yr * lax.rsqrt(jnp.mean(yr * yr, axis=-1, keepdims=True) + EPS) * ng_ref[:, vs_]
                  * _silu(g_ref[rows, vs_]))
            y_ref[rows, vs_] = yn.astype(y_ref.dtype)

            w_end = jnp.where(trow < sr, jnp.exp((sr - 1 - trow).astype(F32) * lg), 0.0)
            kw = ks * jnp.concatenate([w_end] * (dk // LANES), axis=1)
            e_all = jnp.exp(float(sr) * lg)
            S_new[i, u] = Su * jnp.concatenate([e_all] * nrep, axis=1) + _dot_tn(kw.astype(BF16), vs_bf)

    if not single:
        @pl.when(c == nc - 1)
        def _():
            Sout_ref[...] = S_scr[...]


def _ret_call(rows, hb, P, S_in, cos, sin, norm_g, y_dtype):
    n_seq, nc, bb, sr = rows.n_seq, rows.n_chunks, rows.bb, rows.sr
    H = RET_HEADS
    dk, dv = S_in.shape[2], S_in.shape[3]
    shared = S_in.shape[0] == 1 and n_seq > 1
    sb = (lambda b: 0) if shared else (lambda b: b)
    assert H % hb == 0 and not (shared and bb > 1)
    lg = jnp.log1p(-jnp.exp2(-5.0 - jnp.arange(H, dtype=F32)))
    lg = jnp.broadcast_to(lg[:, None, None], (H, 1, LANES))
    ng = norm_g.reshape(1, H * dv)
    nq, nv = H * dk, H * dv
    wk, wv = hb * dk, hb * dv
    q_c, k_c = (lambda h: h), (lambda h: nq // wk + h)
    v_c, g_c = (lambda h: 2 * nq // wv + h), (lambda h: (2 * nq + nv) // wv + h)
    ins = [P, P, P, P, cos, sin, lg, S_in, ng]
    specs = [rows.spec(wk, q_c), rows.spec(wk, k_c), rows.spec(wv, v_c), rows.spec(wv, g_c),
             pl.BlockSpec((sr, dk // 2), lambda b, h, c: (c, 0)),
             pl.BlockSpec((sr, dk // 2), lambda b, h, c: (c, 0)),
             pl.BlockSpec((hb, 1, LANES), lambda b, h, c: (h, 0, 0)),
             pl.BlockSpec((bb, hb, dk, dv), lambda b, h, c: (sb(b), h, 0, 0)),
             pl.BlockSpec((1, wv), lambda b, h, c: (0, h))]
    out_shape = [rows.out_shape(nv, y_dtype), jax.ShapeDtypeStruct((n_seq, H, dk, dv), F32)]
    out_specs = [rows.out_spec(wv), pl.BlockSpec((bb, hb, dk, dv), lambda b, h, c: (b, h, 0, 0))]
    single = nc == 1
    ls = _ls(sr)
    scratch = [] if single else [pltpu.VMEM((bb, hb, dk, dv), F32)]
    if sr < ls:
        n_it = bb * hb
        scratch += [pltpu.VMEM((n_it, ls, dk), F32), pltpu.VMEM((n_it, ls, dk), F32), pltpu.VMEM((n_it, ls, dv), F32)]
    return pl.pallas_call(
        functools.partial(_ret_body, bb=bb, sr=sr, hb=hb, dk=dk, dv=dv, scale=float(dk) ** -0.5, single=single),
        grid=(rows.n_blk, H // hb, nc), in_specs=specs, out_specs=out_specs, out_shape=out_shape,
        scratch_shapes=scratch,
        compiler_params=_params(("parallel", "parallel", "arbitrary")), name="retention")(*ins)


def _rope_tables(pos, half):
    inv = ROPE_BASE ** (-jnp.arange(half, dtype=F32) / half)
    ang = pos.astype(F32)[:, None] * inv
    return jnp.cos(ang), jnp.sin(ang)


SAMPLE_BB = 2
SSD_GB_PROMPT, SSD_GB_SAMPLE = 2, 2
ML_HB_PROMPT, ML_HB_SAMPLE = 8, 4
RET_HB_PROMPT, RET_HB_SAMPLE = 8, 4


def kernel(x_prompt, x_sample, state_ssm, state_ssd_conv, state_mlstm_C, state_mlstm_n, state_mlstm_m, state_ret, meta_tokens, pre_norm_g, post_norm_g, ssd_w_in, ssd_conv_w, ssd_conv_b, ssd_dt_bias, ssd_A_log, ssd_D, ssd_norm_g, ssd_w_out, ml_w_in, ml_b_gates, ml_norm_g, ml_w_out, ret_w_in, ret_norm_g, ret_w_out):
    Bp, Sp, D = x_prompt.shape
    Bd, Ld, _ = x_sample.shape
    n_meta = meta_tokens.shape[0]
    depth = pre_norm_g.shape[0]
    n_p, n_s = Bp * Sp, Bd * Ld
    R = n_p + n_s + n_meta
    assert Sp % CHUNK == 0 and (n_p + n_s) % n_meta == 0 and n_meta % 8 == 0
    assert SAMPLE_BB * Ld == SUBLANES and Bd % SAMPLE_BB == 0

    rows_p = _Rows(Bp, 1, CHUNK, Sp // CHUNK, 0, y_rows=R)
    rows_s = _Rows(Bd // SAMPLE_BB, SAMPLE_BB, Ld, 1, n_p)
    rows_m = _Rows(1, 1, n_meta, 1, n_p + n_s)

    def join(y, ys, ym):
        y = lax.dynamic_update_slice(y, ys.astype(BF16), (n_p, 0))
        return lax.dynamic_update_slice(y, ym, (n_p + n_s, 0))

    x = jnp.concatenate([x_prompt.reshape(n_p, D), x_sample.reshape(n_s, D), meta_tokens.astype(x_prompt.dtype)], axis=0)

    def side_w(w, j, lo, n):
        cols = lax.slice(w, (j, 0, lo), (j + 1, w.shape[1], lo + n)).reshape(w.shape[1], n)
        return jnp.pad(cols, ((0, 0), (0, LANES - n)))

    n_ssd = state_ssm.shape[0]
    H_ssd, P_ssd, N_ssd = state_ssm.shape[2:]
    d_inner = H_ssd * P_ssd
    conv_dim = state_ssd_conv.shape[-1]
    ssd_main = d_inner + conv_dim
    ml_dk, ml_dv = state_mlstm_C.shape[3:]
    ml_main = 2 * ML_HEADS * ml_dk + 3 * ML_HEADS * ml_dv
    ret_dk, ret_dv = state_ret.shape[3:]
    ret_main = 2 * RET_HEADS * ret_dk + 2 * RET_HEADS * ret_dv
    ssm_in_s = state_ssm.reshape(n_ssd, Bd, d_inner, N_ssd)

    def side_for(i):
        kind, j = i % 3, i // 3
        if kind == 0:
            return side_w(ssd_w_in, j, ssd_main, H_ssd)
        if kind == 1:
            return side_w(ml_w_in, j, ml_main, 2 * ML_HEADS)
        return None

    cos_p, sin_p = _rope_tables(n_meta + jnp.arange(Sp), ret_dk // 2)
    cos_s, sin_s = _rope_tables(PAST_LEN + jnp.arange(Ld), ret_dk // 2)
    cos_m, sin_m = _rope_tables(jnp.arange(n_meta), ret_dk // 2)

    ssm_p, conv_p, mC_p, mn_p, mm_p, ret_p = [], [], [], [], [], []
    conv_s, mC_s, mn_s, mm_s, ret_s = [], [], [], [], []
    ssm_s = None

    sw = side_for(0)
    h, side = _norm_call(x, g_pre=pre_norm_g[0], w_side=sw)
    for i in range(depth):
        kind, j = i % 3, i // 3
        if kind == 0:
            P = _matmul(h, ssd_w_in, j, ssd_main)
            args = (ssd_conv_w[j], ssd_conv_b[j], ssd_dt_bias[j], ssd_A_log[j], ssd_D[j], ssd_norm_g[j])
            zc = jnp.zeros((1, 1, SSD_CONV_W - 1, conv_dim), F32)
            zs = jnp.zeros((1, 1, d_inner, N_ssd), F32)
            ym, cx, cB, cC, sm = _ssd_call(rows_m, SSD_GB_PROMPT, P, side[n_p + n_s:], zc, zs, 0, *args, BF16)
            conv_m = jnp.concatenate([cx, cB, cC], axis=-1)
            yp, cx, cB, cC, sp = _ssd_call(rows_p, SSD_GB_PROMPT, P, side[:n_p], conv_m[None], sm[None], 0, *args, BF16)
            conv_p.append(jnp.concatenate([cx, cB, cC], axis=-1))
            ssm_p.append(sp.reshape(Bp, H_ssd, P_ssd, N_ssd))
            ys, cx, cB, cC, ssm_s = _ssd_call(rows_s, SSD_GB_SAMPLE, P, side[n_p:n_p + n_s],
                                              state_ssd_conv, ssm_in_s, j, *args, F32,
                                              ssm_alias=ssm_s, n_layers_out=n_ssd)
            conv_s.append(jnp.concatenate([cx, cB, cC], axis=-1))
            out = _matmul(join(yp, ys, ym), ssd_w_out, j, D)
        elif kind == 1:
            P = _matmul(h, ml_w_in, j, ml_main)
            args = (ml_b_gates[j], ml_norm_g[j])
            z = lambda *s: jnp.zeros(s, F32)
            ym, Cm, nm, mm = _ml_call(rows_m, ML_HB_PROMPT, P, side[n_p + n_s:], z(1, ML_HEADS, ml_dk, ml_dv),
                                      z(1, ML_HEADS, 1, ml_dk), z(1, ML_HEADS, 1, 1), *args, BF16)
            yp, Cp, np_, mp = _ml_call(rows_p, ML_HB_PROMPT, P, side[:n_p], Cm, nm, mm, *args, BF16)
            mC_p.append(Cp)
            mn_p.append(np_.reshape(Bp, ML_HEADS, ml_dk))
            mm_p.append(mp.reshape(Bp, ML_HEADS))
            ys, Cs, ns_, ms = _ml_call(rows_s, ML_HB_SAMPLE, P, side[n_p:n_p + n_s], state_mlstm_C[j],
                                       state_mlstm_n[j].reshape(Bd, ML_HEADS, 1, ml_dk),
                                       state_mlstm_m[j].reshape(Bd, ML_HEADS, 1, 1), *args, F32)
            mC_s.append(Cs)
            mn_s.append(ns_.reshape(Bd, ML_HEADS, ml_dk))
            mm_s.append(ms.reshape(Bd, ML_HEADS))
            out = _matmul(join(yp, ys, ym), ml_w_out, j, D)
        else:
            P = _matmul(h, ret_w_in, j, ret_main)
            ym, Sm = _ret_call(rows_m, RET_HB_PROMPT, P, jnp.zeros((1, RET_HEADS, ret_dk, ret_dv), F32), cos_m, sin_m,
                               ret_norm_g[j], BF16)
            yp, Sp_ = _ret_call(rows_p, RET_HB_PROMPT, P, Sm, cos_p, sin_p, ret_norm_g[j], BF16)
            ret_p.append(Sp_)
            ys, Ss = _ret_call(rows_s, RET_HB_SAMPLE, P, state_ret[j], cos_s, sin_s, ret_norm_g[j], F32)
            ret_s.append(Ss)
            out = _matmul(join(yp, ys, ym), ret_w_out, j, D)
        if i + 1 < depth:
            sw = side_for(i + 1)
            res = _norm_call(x, o=out, g_post=post_norm_g[i], g_pre=pre_norm_g[i + 1], w_side=sw)
            x, h = res[0], res[1]
            side = res[2] if sw is not None else None
        else:
            (y_prompt,) = _norm_call(x, o=out, g_post=post_norm_g[i], row_lo=0, n_rows=n_p)
            (y_sample,) = _norm_call(x, o=out, g_post=post_norm_g[i], row_lo=n_p, n_rows=n_s)

    y_prompt = y_prompt.reshape(Bp, Sp, D)
    y_sample = y_sample.reshape(Bd, Ld, D)
    return (y_prompt, y_sample, _stack(ssm_p), _stack(conv_p), _stack(mC_p), _stack(mn_p), _stack(mm_p), _stack(ret_p),
            ssm_s.reshape(n_ssd, Bd, H_ssd, P_ssd, N_ssd), _stack(conv_s), _stack(mC_s), _stack(mn_s), _stack(mm_s),
            _stack(ret_s))
```

```python
import functools
import math

import jax
import jax.numpy as jnp
from jax import lax
from jax.experimental import pallas as pl
from jax.experimental.pallas import tpu as pltpu

F32 = jnp.float32
BF16 = jnp.bfloat16

EPS = 1e-6
CHUNK = 128
LANES = 128
SUBLANES = 8
PAST_LEN = 16384
ROPE_BASE = 10000.0
K_BLOCK = 4096
NEG_BIG = -1e30
LOG2E = 1.4426950408889634
VMEM_LIMIT = 56 * 1024 * 1024
NORM_ROWS = 256

SSD_GROUPS = 8
SSD_HEAD_DIM = 64
SSD_CONV_W = 4
ML_HEADS = 8
RET_HEADS = 16


def _dot(a, b):
    return jnp.dot(a, b, preferred_element_type=F32)


def _dot_nt(a, b):
    return lax.dot_general(a, b, (((1,), (1,)), ((), ())), preferred_element_type=F32)


def _dot_tn(a, b):
    return lax.dot_general(a, b, (((0,), (0,)), ((), ())), preferred_element_type=F32)


def _split3(x):
    hi = x.astype(BF16)
    r1 = x - hi.astype(F32)
    mid = r1.astype(BF16)
    lo = (r1 - mid.astype(F32)).astype(BF16)
    return hi, mid, lo


def _cum_left(tri, x):
    hi, mid, lo = _split3(x)
    return _dot(tri, hi) + _dot(tri, mid) + _dot(tri, lo)


def _cum_right(x, tri):
    hi, mid, lo = _split3(x)
    return _dot(hi, tri) + _dot(mid, tri) + _dot(lo, tri)


def _tri(n, lower):
    r = lax.broadcasted_iota(jnp.int32, (n, n), 0)
    c = lax.broadcasted_iota(jnp.int32, (n, n), 1)
    m = (r >= c) if lower else (r <= c)
    return jnp.where(m, 1.0, 0.0).astype(BF16)


def _softplus(x):
    return jnp.maximum(x, 0.0) + jnp.log1p(jnp.exp(-jnp.abs(x)))


def _log_sigmoid(x):
    return jnp.minimum(x, 0.0) - jnp.log1p(jnp.exp(-jnp.abs(x)))


def _sigmoid(x):
    return 0.5 + 0.5 * jnp.tanh(0.5 * x)


def _silu(x):
    h = 0.5 * x
    return h + h * jnp.tanh(h)


def _pad_rows(val, pad_ref, fill=0.0):
    if pad_ref is None:
        return val
    sr = val.shape[0]
    pad_ref[...] = jnp.full(pad_ref.shape, fill, pad_ref.dtype)
    pad_ref[0:sr, :] = val
    return pad_ref[...]


def _pad_lanes(val, pad_ref, fill=0.0):
    if pad_ref is None:
        return val
    sr = val.shape[1]
    pad_ref[...] = jnp.full(pad_ref.shape, fill, pad_ref.dtype)
    pad_ref[:, 0:sr] = val
    return pad_ref[...]


def _row_tile(n, cap):
    best = None
    for t in range(16, min(n, cap) + 1, 16):
        if n % t == 0:
            best = t
    return best if best is not None else n


def _params(sem, limit=VMEM_LIMIT):
    return pltpu.CompilerParams(dimension_semantics=sem, vmem_limit_bytes=limit)


def _stack(parts):
    return parts[0][None] if len(parts) == 1 else jnp.stack(parts)


def _norm_body(*refs, has_post, has_pre, has_side):
    refs = list(refs)
    if has_post:
        o_ref, x_ref, gpost_ref = refs[:3]
        refs = refs[3:]
    else:
        x_ref = refs[0]
        refs = refs[1:]
    if has_pre:
        gpre_ref = refs[0]
        refs = refs[1:]
    if has_side:
        ws_ref = refs[0]
        refs = refs[1:]
    outs = refs
    x = x_ref[...]
    if has_post:
        o = o_ref[...]
        x = x + o * lax.rsqrt(jnp.mean(o * o, axis=-1, keepdims=True) + EPS) * gpost_ref[...]
        outs[0][...] = x
        outs = outs[1:]
    if has_pre:
        h = (x * lax.rsqrt(jnp.mean(x * x, axis=-1, keepdims=True) + EPS) * gpre_ref[...]).astype(BF16)
        outs[0][...] = h
        if has_side:
            outs[1][...] = _dot(h, ws_ref[...].astype(BF16))


def _norm_call(x, o=None, g_post=None, g_pre=None, w_side=None, row_lo=0, n_rows=None):
    D = x.shape[1]
    R = x.shape[0] if n_rows is None else n_rows
    has_post, has_pre, has_side = o is not None, g_pre is not None, w_side is not None
    if n_rows is None:
        tr = min(NORM_ROWS, R)
    else:
        tr = _row_tile(math.gcd(R, row_lo) if row_lo else R, NORM_ROWS)
        assert R % tr == 0 and row_lo % tr == 0
    row_in = pl.BlockSpec((tr, D), lambda i: (row_lo // tr + i, 0))
    row = pl.BlockSpec((tr, D), lambda i: (i, 0))
    vec = pl.BlockSpec((1, D), lambda i: (0, 0))
    ins, specs = [], []
    if has_post:
        ins += [o, x, g_post.reshape(1, D)]
        specs += [row_in, row_in, vec]
    else:
        ins += [x]
        specs += [row_in]
    if has_pre:
        ins += [g_pre.reshape(1, D)]
        specs += [vec]
    if has_side:
        ins += [w_side]
        specs += [pl.BlockSpec(w_side.shape, lambda i: (0, 0))]
    out_shape, out_specs = [], []
    if has_post:
        out_shape.append(jax.ShapeDtypeStruct((R, D), F32))
        out_specs.append(row)
    if has_pre:
        out_shape.append(jax.ShapeDtypeStruct((R, D), BF16))
        out_specs.append(row)
        if has_side:
            out_shape.append(jax.ShapeDtypeStruct((R, w_side.shape[1]), F32))
            out_specs.append(pl.BlockSpec((tr, w_side.shape[1]), lambda i: (i, 0)))
    return pl.pallas_call(
        functools.partial(_norm_body, has_post=has_post, has_pre=has_pre, has_side=has_side),
        grid=(pl.cdiv(R, tr),), in_specs=specs, out_specs=out_specs, out_shape=out_shape,
        compiler_params=_params(("parallel",)), name="norm")(*ins)


def _mm_body(x_ref, w_ref, o_ref):
    o_ref[...] = _dot(x_ref[...], w_ref[...].astype(BF16))


def _mm_acc_body(x_ref, w_ref, a_ref, o_ref):
    o_ref[...] = a_ref[...] + _dot(x_ref[...], w_ref[...].astype(BF16))


def _matmul(x, w, layer, n_cols):
    R, K = x.shape
    tm = _row_tile(R, 2048)
    tn = 512
    out = None
    for kb in range(K // K_BLOCK):
        x_spec = pl.BlockSpec((tm, K_BLOCK), lambda i, j, kb=kb: (i, kb), pipeline_mode=pl.Buffered(1))
        w_spec = pl.BlockSpec((None, K_BLOCK, tn), lambda i, j, kb=kb: (layer, kb, j))
        o_spec = pl.BlockSpec((tm, tn), lambda i, j: (i, j))
        shape = jax.ShapeDtypeStruct((R, n_cols), F32)
        if out is None:
            out = pl.pallas_call(_mm_body, grid=(R // tm, n_cols // tn), in_specs=[x_spec, w_spec],
                                 out_specs=o_spec, out_shape=shape,
                                 compiler_params=_params(("parallel", "arbitrary")), name="proj")(x, w)
        else:
            out = pl.pallas_call(_mm_acc_body, grid=(R // tm, n_cols // tn), in_specs=[x_spec, w_spec, o_spec],
                                 out_specs=o_spec, out_shape=shape, input_output_aliases={2: 0},
                                 compiler_params=_params(("parallel", "arbitrary")), name="proj_acc")(x, w, out)
    return out


class _Rows:
    def __init__(self, n_blk, bb, sr, n_chunks, row_off=0, y_rows=None):
        self.n_blk, self.bb, self.sr, self.n_chunks, self.row_off, self.y_rows = n_blk, bb, sr, n_chunks, row_off, y_rows
        self.rb = bb * sr
        self.n_seq = n_blk * bb
        assert row_off % self.rb == 0 and (bb == 1 or n_chunks == 1)

    def spec(self, width, col_fn):
        rb, nc, off = self.rb, self.n_chunks, self.row_off // self.rb
        return pl.BlockSpec((rb, width), lambda b, h, c: (off + b * nc + c, col_fn(h)))

    def prev_spec(self, width, col_fn):
        nc, k = self.n_chunks, self.rb // SUBLANES
        return pl.BlockSpec((SUBLANES, width), lambda b, h, c: (jnp.maximum((b * nc + c) * k - 1, 0), col_fn(h)))

    def out_shape(self, width, dtype):
        rows = self.y_rows if self.y_rows is not None else self.n_blk * self.n_chunks * self.rb
        return jax.ShapeDtypeStruct((rows, width), dtype)

    def out_spec(self, width):
        rb, nc = self.rb, self.n_chunks
        return pl.BlockSpec((rb, width), lambda b, h, c: (b * nc + c, h))


def _ls(sr):
    return max(SUBLANES, sr)


def _ssd_body(*refs, bb, sr, has_prev, gb, gw, has_alias):
    ls = _ls(sr)
    single = not has_prev
    it = iter(refs)
    z_ref, x_ref, B_ref, C_ref = next(it), next(it), next(it), next(it)
    if has_prev:
        xp_ref, Bp_ref, Cp_ref = next(it), next(it), next(it)
    else:
        xp_ref = Bp_ref = Cp_ref = None
    csx_ref, csB_ref, csC_ref = next(it), next(it), next(it)
    cwx_ref, cwB_ref, cwC_ref = next(it), next(it), next(it)
    cbx_ref, cbB_ref, cbC_ref = next(it), next(it), next(it)
    dtc_ref, dtr_ref, hpr_ref, hpc_ref, D_ref, ng_ref, s0_ref = (next(it) for _ in range(7))
    if has_alias:
        next(it)
    y_ref, ocx_ref, ocB_ref, ocC_ref, sout_ref = (next(it) for _ in range(5))
    S_scr = None if single else next(it)
    ex_all, eB_all, eC_all, xw_all, y_all = (next(it) for _ in range(5))
    pads_all = [next(it) for _ in range(5)] if sr < ls else None

    c = pl.program_id(2)
    nc = pl.num_programs(2)

    if not single:
        @pl.when(c == 0)
        def _():
            S_scr[...] = s0_ref[...]

    row = lax.broadcasted_iota(jnp.int32, (ls, ls), 0)
    col = lax.broadcasted_iota(jnp.int32, (ls, ls), 1)
    causal = row >= col
    lo_s = lax.broadcasted_iota(jnp.int32, (ls, LANES), 1) < SSD_HEAD_DIM
    tri_l, tri_u = _tri(ls, True), _tri(ls, False)
    hg = gw // SSD_HEAD_DIM

    for i in range(bb):
        rows = slice(i * sr, (i + 1) * sr)
        S_old = s0_ref.at[i] if single else S_scr.at[i]
        S_new = sout_ref.at[i] if single else S_scr.at[i]

        def conv(cur_ref, prev_ref, cs_ref, w_ref, b_ref, ext, oc_ref):
            if has_prev:
                prev3 = jnp.where(c == 0, cs_ref[i], prev_ref[5:8, :])
            else:
                prev3 = cs_ref[i]
            ext[5:8, :] = prev3
            ext[8:8 + sr, :] = cur_ref[rows, :]
            if sr % SUBLANES == 0:
                v = ext[...]
                acc = b_ref[...] + v[8:8 + sr, :] * w_ref[3:4, :]
                for k in range(SSD_CONV_W - 1):
                    acc = acc + pltpu.roll(v, SSD_CONV_W - 1 - k, axis=0)[8:8 + sr, :] * w_ref[k:k + 1, :]
            else:
                acc = b_ref[...] + ext[5:5 + sr, :] * w_ref[0:1, :]
                for k in range(1, SSD_CONV_W):
                    acc = acc + ext[5 + k:5 + k + sr, :] * w_ref[k:k + 1, :]
            oc_ref[i] = ext[5 + sr:8 + sr, :]
            return _silu(acc)

        x_all = conv(x_ref, xp_ref, csx_ref, cwx_ref, cbx_ref, ex_all.at[i], ocx_ref)
        B_all = conv(B_ref, Bp_ref, csB_ref, cwB_ref, cbB_ref, eB_all.at[i], ocB_ref)
        C_all = conv(C_ref, Cp_ref, csC_ref, cwC_ref, cbC_ref, eC_all.at[i], ocC_ref)

        for u in range(gb):
            gs = slice(u * gw, (u + 1) * gw)
            ns = slice(u * LANES, (u + 1) * LANES)
            slot = i * gb + u
            xw_scr, y_scr = xw_all.at[slot], y_all.at[slot]
            if pads_all is None:
                xpad = Bpad = Cpad = dtcpad = dtrpad = None
            else:
                xpad, Bpad, Cpad, dtcpad, dtrpad = (p.at[slot] for p in pads_all)
            xs = _pad_rows(x_all[:, gs], xpad)
            Bs = _pad_rows(B_all[:, ns], Bpad)
            Cs = _pad_rows(C_all[:, ns], Cpad)

            hpr = hpr_ref[u]
            hpc = hpc_ref[u]
            dtc = _pad_rows(_softplus(dtc_ref[u, i] + hpr[0:1, :]), dtcpad)
            dtr = _pad_lanes(_softplus(dtr_ref[u, i] + hpc[:, 0:1]), dtrpad)
            a_c = dtc * (-LOG2E * jnp.exp(hpr[1:2, :]))
            a_r = dtr * (-LOG2E * jnp.exp(hpc[:, 1:2]))
            cum_c = _cum_left(tri_l, a_c)
            cum_r = _cum_right(a_r, tri_u)
            tot_r = jnp.sum(a_r, axis=1, keepdims=True)
            tot_c = cum_c[ls - 1:ls, :]
            e_c = jnp.exp2(cum_c)
            te_c = jnp.exp2(tot_c - cum_c) * dtc
            r2 = cum_r - jnp.log2(dtr)

            Bs_bf = Bs.astype(BF16)
            Cs_bf = Cs.astype(BF16)
            cb = _dot_nt(Cs_bf, Bs_bf)
            Yi = _dot_nt(Cs_bf, S_old[gs, :].astype(BF16))

            for jj in range(hg // 2):
                j0, j1 = 2 * jj, 2 * jj + 1
                sl = slice(LANES * jj, LANES * (jj + 1))
                ws = []
                for j in (j0, j1):
                    seg = cum_c[:, j:j + 1] - r2[j:j + 1, :]
                    ws.append((jnp.exp2(jnp.where(causal, seg, -jnp.inf)) * cb).astype(BF16))
                xp = xs[:, sl]
                xp_bf = xp.astype(BF16)
                if ls == CHUNK:
                    both = _dot(jnp.concatenate(ws, axis=0), xp_bf)
                    intra = jnp.where(lo_s, both[0:ls, :], both[ls:2 * ls, :])
                else:
                    intra = jnp.where(lo_s, _dot(ws[0], xp_bf), _dot(ws[1], xp_bf))
                e_pair = jnp.where(lo_s, e_c[:, j0:j0 + 1], e_c[:, j1:j1 + 1])
                dsl = slice(u * gw + LANES * jj, u * gw + LANES * (jj + 1))
                y_scr[:, sl] = intra + Yi[:, sl] * e_pair + xp * D_ref[:, dsl]
                te_pair = jnp.where(lo_s, te_c[:, j0:j0 + 1], te_c[:, j1:j1 + 1])
                xw_scr[:, sl] = (xp * te_pair).astype(xw_scr.dtype)

            yg = y_scr[0:sr, :] * _silu(z_ref[rows, gs])
            yn = yg * lax.rsqrt(jnp.mean(yg * yg, axis=-1, keepdims=True) + EPS) * ng_ref[:, gs]
            y_ref[rows, gs] = yn.astype(y_ref.dtype)

            upd = _dot_tn(xw_scr[...].astype(BF16), Bs_bf)
            etot = jnp.broadcast_to(jnp.exp2(tot_r), (hg, LANES))
            for j in range(hg):
                rs = slice(u * gw + SSD_HEAD_DIM * j, u * gw + SSD_HEAD_DIM * (j + 1))
                us = slice(SSD_HEAD_DIM * j, SSD_HEAD_DIM * (j + 1))
                S_new[rs, :] = S_old[rs, :] * etot[j:j + 1, :] + upd[us, :]

    if not single:
        @pl.when(c == nc - 1)
        def _():
            sout_ref[...] = S_scr[...]


def _ssd_call(rows, gb, P, side, conv_in, ssm_in, layer, conv_w, conv_b, dt_bias, A_log, d_skip, norm_g, y_dtype,
              ssm_alias=None, n_layers_out=None):
    n_seq, nc, bb, sr = rows.n_seq, rows.n_chunks, rows.bb, rows.sr
    L = nc * sr
    G = SSD_GROUPS
    d_inner = d_skip.shape[0] * SSD_HEAD_DIM
    gw = d_inner // G
    n_state = ssm_in.shape[-1]
    hg = gw // SSD_HEAD_DIM
    shared = conv_in.shape[1] == 1 and n_seq > 1
    sb = (lambda b: 0) if shared else (lambda b: b)
    assert G % gb == 0 and n_state == LANES and not (shared and bb > 1)

    s4 = side.reshape(n_seq, L, G, hg)
    dtc = jnp.pad(jnp.transpose(s4, (2, 0, 1, 3)), ((0, 0), (0, 0), (0, 0), (0, LANES - hg)))
    dtr = jnp.transpose(s4, (2, 0, 3, 1))
    hp = jnp.stack([dt_bias.reshape(G, hg), A_log.reshape(G, hg)], axis=1)
    hpr = jnp.pad(hp, ((0, 0), (0, 0), (0, LANES - hg)))
    hpc = jnp.transpose(hp, (0, 2, 1))
    d_exp = jnp.repeat(d_skip, SSD_HEAD_DIM).reshape(1, d_inner)
    ng = norm_g.reshape(1, d_inner)
    cb2 = conv_b.reshape(1, -1)

    wx, wn = gb * gw, gb * LANES
    xo, Bo, Co = d_inner // wx, 2 * d_inner // wn, (2 * d_inner + G * n_state) // wn
    z_c, x_c = (lambda g: g), (lambda g: xo + g)
    B_c, C_c = (lambda g: Bo + g), (lambda g: Co + g)
    cxo, cBo, cCo = (lambda g: g), (lambda g: d_inner // wn + g), (lambda g: (d_inner + G * n_state) // wn + g)
    has_prev = nc > 1

    ins = [P, P, P, P]
    specs = [rows.spec(wx, z_c), rows.spec(wx, x_c), rows.spec(wn, B_c), rows.spec(wn, C_c)]
    if has_prev:
        ins += [P, P, P]
        specs += [rows.prev_spec(wx, x_c), rows.prev_spec(wn, B_c), rows.prev_spec(wn, C_c)]
    ins += [conv_in, conv_in, conv_in, conv_w, conv_w, conv_w, cb2, cb2, cb2]
    specs += [pl.BlockSpec((None, bb, 3, wx), lambda b, g, c: (layer, sb(b), 0, cxo(g))),
              pl.BlockSpec((None, bb, 3, wn), lambda b, g, c: (layer, sb(b), 0, cBo(g))),
              pl.BlockSpec((None, bb, 3, wn), lambda b, g, c: (layer, sb(b), 0, cCo(g))),
              pl.BlockSpec((SSD_CONV_W, wx), lambda b, g, c: (0, cxo(g))),
              pl.BlockSpec((SSD_CONV_W, wn), lambda b, g, c: (0, cBo(g))),
              pl.BlockSpec((SSD_CONV_W, wn), lambda b, g, c: (0, cCo(g))),
              pl.BlockSpec((1, wx), lambda b, g, c: (0, cxo(g))),
              pl.BlockSpec((1, wn), lambda b, g, c: (0, cBo(g))),
              pl.BlockSpec((1, wn), lambda b, g, c: (0, cCo(g)))]
    ins += [dtc, dtr, hpr, hpc, d_exp, ng, ssm_in]
    specs += [pl.BlockSpec((gb, bb, sr, LANES), lambda b, g, c: (g, b, c, 0)),
              pl.BlockSpec((gb, bb, hg, sr), lambda b, g, c: (g, b, 0, c)),
              pl.BlockSpec((gb, 2, LANES), lambda b, g, c: (g, 0, 0)),
              pl.BlockSpec((gb, hg, 2), lambda b, g, c: (g, 0, 0)),
              pl.BlockSpec((1, wx), lambda b, g, c: (0, g)),
              pl.BlockSpec((1, wx), lambda b, g, c: (0, g)),
              pl.BlockSpec((None, bb, wx, n_state), lambda b, g, c: (layer, sb(b), g, 0))]
    aliases = {}
    if ssm_alias is not None:
        aliases = {len(ins): 4}
        ins += [ssm_alias]
        specs += [pl.BlockSpec(memory_space=pl.ANY)]
    if n_layers_out is None:
        s_shape = jax.ShapeDtypeStruct((n_seq, d_inner, n_state), F32)
        s_spec = pl.BlockSpec((bb, wx, n_state), lambda b, g, c: (b, g, 0))
    else:
        s_shape = jax.ShapeDtypeStruct((n_layers_out, n_seq, d_inner, n_state), F32)
        s_spec = pl.BlockSpec((None, bb, wx, n_state), lambda b, g, c: (layer, b, g, 0))
    out_shape = [rows.out_shape(d_inner, y_dtype),
                 jax.ShapeDtypeStruct((n_seq, 3, d_inner), F32),
                 jax.ShapeDtypeStruct((n_seq, 3, G * n_state), F32),
                 jax.ShapeDtypeStruct((n_seq, 3, G * n_state), F32),
                 s_shape]
    out_specs = [rows.out_spec(wx),
                 pl.BlockSpec((bb, 3, wx), lambda b, g, c: (b, 0, g)),
                 pl.BlockSpec((bb, 3, wn), lambda b, g, c: (b, 0, g)),
                 pl.BlockSpec((bb, 3, wn), lambda b, g, c: (b, 0, g)),
                 s_spec]
    ls = _ls(sr)
    scratch = [] if not has_prev else [pltpu.VMEM((bb, wx, n_state), F32)]
    n_it = bb * gb
    scratch += [pltpu.VMEM((bb, 8 + sr, wx), F32), pltpu.VMEM((bb, 8 + sr, wn), F32), pltpu.VMEM((bb, 8 + sr, wn), F32),
                pltpu.VMEM((n_it, ls, gw), BF16 if ls == CHUNK else F32), pltpu.VMEM((n_it, ls, gw), F32)]
    if sr < ls:
        scratch += [pltpu.VMEM((n_it, ls, gw), F32), pltpu.VMEM((n_it, ls, LANES), F32),
                    pltpu.VMEM((n_it, ls, LANES), F32), pltpu.VMEM((n_it, ls, LANES), F32),
                    pltpu.VMEM((n_it, hg, ls), F32)]
    return pl.pallas_call(
        functools.partial(_ssd_body, bb=bb, sr=sr, has_prev=has_prev, gb=gb, gw=gw, has_alias=ssm_alias is not None),
        grid=(rows.n_blk, G // gb, nc), in_specs=specs, out_specs=out_specs, out_shape=out_shape,
        scratch_shapes=scratch, input_output_aliases=aliases,
        compiler_params=_params(("parallel", "parallel", "arbitrary")), name="ssd")(*ins)


def _ml_body(*refs, bb, sr, hb, dk, dv, scale, single):
    ls = _ls(sr)
    it = iter(refs)
    q_ref, k_ref, v_ref, o_ref, z_ref, gc_ref, gr_ref, bgr_ref, bgc_ref, C0_ref, n0_ref, m0_ref, ng_ref = (
        next(it) for _ in range(13))
    h_ref, Cout_ref, nout_ref, mout_ref = (next(it) for _ in range(4))
    if single:
        C_old, n_old, m_old, C_new, n_new, m_new_ref = C0_ref, n0_ref, m0_ref, Cout_ref, nout_ref, mout_ref
    else:
        C_scr, n_scr, m_scr = next(it), next(it), next(it)
        C_old, n_old, m_old, C_new, n_new, m_new_ref = C_scr, n_scr, m_scr, C_scr, n_scr, m_scr
    pads_all = [next(it) for _ in range(6)] if sr < ls else None

    c = pl.program_id(2)
    nc = pl.num_programs(2)

    if not single:
        @pl.when(c == 0)
        def _():
            C_scr[...] = C0_ref[...]
            n_scr[...] = n0_ref[...]
            m_scr[...] = m0_ref[...]

    row = lax.broadcasted_iota(jnp.int32, (ls, ls), 0)
    col = lax.broadcasted_iota(jnp.int32, (ls, ls), 1)
    causal = row >= col
    tri_l, tri_u = _tri(ls, True), _tri(ls, False)

    for i in range(bb):
        rows = slice(i * sr, (i + 1) * sr)
        for u in range(hb):
            ks_ = slice(u * dk, (u + 1) * dk)
            vs_ = slice(u * dv, (u + 1) * dv)
            if pads_all is None:
                kpad = vpad = qpad = igpad = lfpad = grpad = None
            else:
                kpad, vpad, qpad, igpad, lfpad, grpad = (p.at[i * hb + u] for p in pads_all)
            gc = gc_ref[u, i] + bgr_ref[u]
            ig_c = _pad_rows(jnp.broadcast_to(gc[:, 0:1], (sr, LANES)), igpad, NEG_BIG)
            lf_c = _pad_rows(jnp.broadcast_to(_log_sigmoid(gc[:, 1:2]), (sr, LANES)), lfpad)
            gr = gr_ref[u, i] + bgc_ref[u]
            rsel = lax.broadcasted_iota(jnp.int32, gr.shape, 0)
            gr = jnp.where(rsel == 1, _log_sigmoid(gr), gr)
            if grpad is not None:
                grpad[...] = jnp.zeros(grpad.shape, F32)
                grpad[0:1, :] = jnp.full((1, ls), NEG_BIG, F32)
                grpad[:, 0:sr] = gr
                gr = grpad[...]
            ig_r = gr[0:1, :]
            bcum_c = _cum_left(tri_l, lf_c)
            bcum_r = _cum_right(gr, tri_u)[1:2, :]

            logd = jnp.where(causal, bcum_c[:, 0:ls] - bcum_r + ig_r, -jnp.inf)
            m_prev = m_old[i, u]
            m_inter = bcum_c[:, 0:1] + m_prev
            m_t = jnp.maximum(m_inter, jnp.max(logd, axis=1, keepdims=True))
            dmat = jnp.exp(logd - m_t)
            w_prev = jnp.exp(m_inter - m_t)

            q = _pad_rows(q_ref[rows, ks_], qpad)
            ks = _pad_rows(k_ref[rows, ks_], kpad) * scale
            vs_bf = _pad_rows(v_ref[rows, vs_], vpad).astype(BF16)
            q_bf = q.astype(BF16)
            Cu = C_old[i, u]
            nu = n_old[i, u]
            s = _dot_nt(q_bf, ks.astype(BF16)) * dmat
            num = _dot(s.astype(BF16), vs_bf) + w_prev * _dot(q_bf, Cu.astype(BF16))
            den = jnp.sum(s, axis=1, keepdims=True) + w_prev * jnp.sum(q * nu, axis=1, keepdims=True)
            hout = num / jnp.maximum(jnp.abs(den), jnp.exp(-m_t))

            hc = hout[0:sr, :] * _sigmoid(o_ref[rows, vs_])
            hn = (hc * lax.rsqrt(jnp.mean(hc * hc, axis=-1, keepdims=True) + EPS) * ng_ref[:, vs_]
                  * _silu(z_ref[rows, vs_]))
            h_ref[rows, vs_] = hn.astype(h_ref.dtype)

            m_new = m_t[sr - 1:sr, :]
            bc_last = bcum_c[sr - 1:sr, 0:1]
            w_end = jnp.exp(bc_last - bcum_c + ig_c - m_new)
            dec = jnp.exp(bc_last + m_prev - m_new)
            kw = ks * jnp.concatenate([w_end] * (dk // LANES), axis=1)
            C_new[i, u] = dec * Cu + _dot_tn(kw.astype(BF16), vs_bf)
            n_new[i, u] = dec * nu + jnp.sum(kw, axis=0, keepdims=True)
            m_new_ref[i, u] = m_new

    if not single:
        @pl.when(c == nc - 1)
        def _():
            Cout_ref[...] = C_scr[...]
            nout_ref[...] = n_scr[...]
            mout_ref[...] = m_scr[...]


def _ml_call(rows, hb, P, side, C_in, n_in, m_in, b_gates, norm_g, y_dtype):
    n_seq, nc, bb, sr = rows.n_seq, rows.n_chunks, rows.bb, rows.sr
    L = nc * sr
    H = ML_HEADS
    dk, dv = C_in.shape[2], C_in.shape[3]
    shared = C_in.shape[0] == 1 and n_seq > 1
    sb = (lambda b: 0) if shared else (lambda b: b)
    assert H % hb == 0 and not (shared and bb > 1)

    g4 = side[:, :2 * H].reshape(n_seq, L, 2, H)
    gc = jnp.pad(jnp.transpose(g4, (3, 0, 1, 2)), ((0, 0), (0, 0), (0, 0), (0, LANES - 2)))
    gr = jnp.pad(jnp.transpose(g4, (3, 0, 2, 1)), ((0, 0), (0, 0), (0, 6), (0, 0)))
    bg = jnp.transpose(b_gates.reshape(2, H))
    bgr = jnp.pad(bg, ((0, 0), (0, LANES - 2))).reshape(H, 1, LANES)
    bgc = jnp.pad(bg, ((0, 0), (0, 6))).reshape(H, 8, 1)
    ng = norm_g.reshape(1, H * dv)

    nq, nv = H * dk, H * dv
    wk, wv = hb * dk, hb * dv
    q_c, k_c = (lambda h: h), (lambda h: nq // wk + h)
    v_c, o_c, z_c = (lambda h: 2 * nq // wv + h), (lambda h: (2 * nq + nv) // wv + h), (lambda h: (2 * nq + 2 * nv) // wv + h)
    ins = [P, P, P, P, P, gc, gr, bgr, bgc, C_in, n_in, m_in, ng]
    specs = [rows.spec(wk, q_c), rows.spec(wk, k_c), rows.spec(wv, v_c), rows.spec(wv, o_c), rows.spec(wv, z_c),
             pl.BlockSpec((hb, bb, sr, LANES), lambda b, h, c: (h, b, c, 0)),
             pl.BlockSpec((hb, bb, 8, sr), lambda b, h, c: (h, b, 0, c)),
             pl.BlockSpec((hb, 1, LANES), lambda b, h, c: (h, 0, 0)),
             pl.BlockSpec((hb, 8, 1), lambda b, h, c: (h, 0, 0)),
             pl.BlockSpec((bb, hb, dk, dv), lambda b, h, c: (sb(b), h, 0, 0)),
             pl.BlockSpec((bb, hb, 1, dk), lambda b, h, c: (sb(b), h, 0, 0)),
             pl.BlockSpec((bb, hb, 1, 1), lambda b, h, c: (sb(b), h, 0, 0)),
             pl.BlockSpec((1, wv), lambda b, h, c: (0, h))]
    out_shape = [rows.out_shape(nv, y_dtype),
                 jax.ShapeDtypeStruct((n_seq, H, dk, dv), F32),
                 jax.ShapeDtypeStruct((n_seq, H, 1, dk), F32),
                 jax.ShapeDtypeStruct((n_seq, H, 1, 1), F32)]
    out_specs = [rows.out_spec(wv),
                 pl.BlockSpec((bb, hb, dk, dv), lambda b, h, c: (b, h, 0, 0)),
                 pl.BlockSpec((bb, hb, 1, dk), lambda b, h, c: (b, h, 0, 0)),
                 pl.BlockSpec((bb, hb, 1, 1), lambda b, h, c: (b, h, 0, 0))]
    single = nc == 1
    ls = _ls(sr)
    scratch = [] if single else [pltpu.VMEM((bb, hb, dk, dv), F32), pltpu.VMEM((bb, hb, 1, dk), F32),
                                 pltpu.VMEM((bb, hb, 1, 1), F32)]
    if sr < ls:
        n_it = bb * hb
        scratch += [pltpu.VMEM((n_it, ls, dk), F32), pltpu.VMEM((n_it, ls, dv), F32), pltpu.VMEM((n_it, ls, dk), F32),
                    pltpu.VMEM((n_it, ls, LANES), F32), pltpu.VMEM((n_it, ls, LANES), F32),
                    pltpu.VMEM((n_it, 8, ls), F32)]
    return pl.pallas_call(
        functools.partial(_ml_body, bb=bb, sr=sr, hb=hb, dk=dk, dv=dv, scale=float(dk) ** -0.5, single=single),
        grid=(rows.n_blk, H // hb, nc), in_specs=specs, out_specs=out_specs, out_shape=out_shape,
        scratch_shapes=scratch,
        compiler_params=_params(("parallel", "parallel", "arbitrary")), name="mlstm")(*ins)


def _ret_body(*refs, bb, sr, hb, dk, dv, scale, single):
    ls = _ls(sr)
    it = iter(refs)
    q_ref, k_ref, v_ref, g_ref, cos_ref, sin_ref, lg_ref, S0_ref, ng_ref = (next(it) for _ in range(9))
    y_ref, Sout_ref = next(it), next(it)
    if single:
        S_old, S_new = S0_ref, Sout_ref
    else:
        S_scr = next(it)
        S_old = S_new = S_scr
    pads_all = [next(it) for _ in range(3)] if sr < ls else None

    c = pl.program_id(2)
    nc = pl.num_programs(2)

    if not single:
        @pl.when(c == 0)
        def _():
            S_scr[...] = S0_ref[...]

    cos, sin = cos_ref[...], sin_ref[...]
    half = cos.shape[1]

    def rope(x):
        x1, x2 = x[:, :half], x[:, half:]
        return jnp.concatenate([x1 * cos - x2 * sin, x1 * sin + x2 * cos], axis=1)

    row = lax.broadcasted_iota(jnp.int32, (ls, ls), 0)
    col = lax.broadcasted_iota(jnp.int32, (ls, ls), 1)
    causal = row >= col
    rel = (row - col).astype(F32)
    trow = lax.broadcasted_iota(jnp.int32, (ls, LANES), 0)
    nrep = dv // LANES

    for i in range(bb):
        rows = slice(i * sr, (i + 1) * sr)
        for u in range(hb):
            ks_ = slice(u * dk, (u + 1) * dk)
            vs_ = slice(u * dv, (u + 1) * dv)
            if pads_all is None:
                qpad = kpad = vpad = None
            else:
                qpad, kpad, vpad = (p.at[i * hb + u] for p in pads_all)
            q_bf = _pad_rows(rope(q_ref[rows, ks_]), qpad).astype(BF16)
            ks = _pad_rows(rope(k_ref[rows, ks_]) * scale, kpad)
            vs_bf = _pad_rows(v_ref[rows, vs_], vpad).astype(BF16)
            lg = lg_ref[u]
            Su = S_old[i, u]

            dmat = jnp.exp(jnp.where(causal, rel * lg[:, 0:ls], -jnp.inf))
            a = _dot_nt(q_bf, ks.astype(BF16)) * dmat
            e_t = jnp.exp((trow + 1).astype(F32) * lg)
            y = _dot(a.astype(BF16), vs_bf) + _dot(q_bf, Su.astype(BF16)) * jnp.concatenate([e_t] * nrep, axis=1)

            yr = y[0:sr, :]
            yn = (yr * lax.rsqrt(jnp.mean(yr * yr, axis=-1, keepdims=True) + EPS) * ng_ref[:, vs_]
                  * _silu(g_ref[rows, vs_]))
            y_ref[rows, vs_] = yn.astype(y_ref.dtype)

            w_end = jnp.where(trow < sr, jnp.exp((sr - 1 - trow).astype(F32) * lg), 0.0)
            kw = ks * jnp.concatenate([w_end] * (dk // LANES), axis=1)
            e_all = jnp.exp(float(sr) * lg)
            S_new[i, u] = Su * jnp.concatenate([e_all] * nrep, axis=1) + _dot_tn(kw.astype(BF16), vs_bf)

    if not single:
        @pl.when(c == nc - 1)
        def _():
            Sout_ref[...] = S_scr[...]


def _ret_call(rows, hb, P, S_in, cos, sin, norm_g, y_dtype):
    n_seq, nc, bb, sr = rows.n_seq, rows.n_chunks, rows.bb, rows.sr
    H = RET_HEADS
    dk, dv = S_in.shape[2], S_in.shape[3]
    shared = S_in.shape[0] == 1 and n_seq > 1
    sb = (lambda b: 0) if shared else (lambda b: b)
    assert H % hb == 0 and not (shared and bb > 1)
    lg = jnp.log1p(-jnp.exp2(-5.0 - jnp.arange(H, dtype=F32)))
    lg = jnp.broadcast_to(lg[:, None, None], (H, 1, LANES))
    ng = norm_g.reshape(1, H * dv)
    nq, nv = H * dk, H * dv
    wk, wv = hb * dk, hb * dv
    q_c, k_c = (lambda h: h), (lambda h: nq // wk + h)
    v_c, g_c = (lambda h: 2 * nq // wv + h), (lambda h: (2 * nq + nv) // wv + h)
    ins = [P, P, P, P, cos, sin, lg, S_in, ng]
    specs = [rows.spec(wk, q_c), rows.spec(wk, k_c), rows.spec(wv, v_c), rows.spec(wv, g_c),
             pl.BlockSpec((sr, dk // 2), lambda b, h, c: (c, 0)),
             pl.BlockSpec((sr, dk // 2), lambda b, h, c: (c, 0)),
             pl.BlockSpec((hb, 1, LANES), lambda b, h, c: (h, 0, 0)),
             pl.BlockSpec((bb, hb, dk, dv), lambda b, h, c: (sb(b), h, 0, 0)),
             pl.BlockSpec((1, wv), lambda b, h, c: (0, h))]
    out_shape = [rows.out_shape(nv, y_dtype), jax.ShapeDtypeStruct((n_seq, H, dk, dv), F32)]
    out_specs = [rows.out_spec(wv), pl.BlockSpec((bb, hb, dk, dv), lambda b, h, c: (b, h, 0, 0))]
    single = nc == 1
    ls = _ls(sr)
    scratch = [] if single else [pltpu.VMEM((bb, hb, dk, dv), F32)]
    if sr < ls:
        n_it = bb * hb
        scratch += [pltpu.VMEM((n_it, ls, dk), F32), pltpu.VMEM((n_it, ls, dk), F32), pltpu.VMEM((n_it, ls, dv), F32)]
    return pl.pallas_call(
        functools.partial(_ret_body, bb=bb, sr=sr, hb=hb, dk=dk, dv=dv, scale=float(dk) ** -0.5, single=single),
        grid=(rows.n_blk, H // hb, nc), in_specs=specs, out_specs=out_specs, out_shape=out_shape,
        scratch_shapes=scratch,
        compiler_params=_params(("parallel", "parallel", "arbitrary")), name="retention")(*ins)


def _rope_tables(pos, half):
    inv = ROPE_BASE ** (-jnp.arange(half, dtype=F32) / half)
    ang = pos.astype(F32)[:, None] * inv
    return jnp.cos(ang), jnp.sin(ang)


SAMPLE_BB = 2
SSD_GB_PROMPT, SSD_GB_SAMPLE = 2, 4
ML_HB_PROMPT, ML_HB_SAMPLE = 8, 8
RET_HB_PROMPT, RET_HB_SAMPLE = 8, 4


def kernel(x_prompt, x_sample, state_ssm, state_ssd_conv, state_mlstm_C, state_mlstm_n, state_mlstm_m, state_ret, meta_tokens, pre_norm_g, post_norm_g, ssd_w_in, ssd_conv_w, ssd_conv_b, ssd_dt_bias, ssd_A_log, ssd_D, ssd_norm_g, ssd_w_out, ml_w_in, ml_b_gates, ml_norm_g, ml_w_out, ret_w_in, ret_norm_g, ret_w_out):
    Bp, Sp, D = x_prompt.shape
    Bd, Ld, _ = x_sample.shape
    n_meta = meta_tokens.shape[0]
    depth = pre_norm_g.shape[0]
    n_p, n_s = Bp * Sp, Bd * Ld
    R = n_p + n_s + n_meta
    assert Sp % CHUNK == 0 and (n_p + n_s) % n_meta == 0 and n_meta % 8 == 0
    assert SAMPLE_BB * Ld == SUBLANES and Bd % SAMPLE_BB == 0

    rows_p = _Rows(Bp, 1, CHUNK, Sp // CHUNK, 0, y_rows=R)
    rows_s = _Rows(Bd // SAMPLE_BB, SAMPLE_BB, Ld, 1, n_p)
    rows_m = _Rows(1, 1, n_meta, 1, n_p + n_s)

    def join(y, ys, ym):
        y = lax.dynamic_update_slice(y, ys.astype(BF16), (n_p, 0))
        return lax.dynamic_update_slice(y, ym, (n_p + n_s, 0))

    x = jnp.concatenate([x_prompt.reshape(n_p, D), x_sample.reshape(n_s, D), meta_tokens.astype(x_prompt.dtype)], axis=0)

    def side_w(w, j, lo, n):
        cols = lax.slice(w, (j, 0, lo), (j + 1, w.shape[1], lo + n)).reshape(w.shape[1], n)
        return jnp.pad(cols, ((0, 0), (0, LANES - n)))

    n_ssd = state_ssm.shape[0]
    H_ssd, P_ssd, N_ssd = state_ssm.shape[2:]
    d_inner = H_ssd * P_ssd
    conv_dim = state_ssd_conv.shape[-1]
    ssd_main = d_inner + conv_dim
    ml_dk, ml_dv = state_mlstm_C.shape[3:]
    ml_main = 2 * ML_HEADS * ml_dk + 3 * ML_HEADS * ml_dv
    ret_dk, ret_dv = state_ret.shape[3:]
    ret_main = 2 * RET_HEADS * ret_dk + 2 * RET_HEADS * ret_dv
    ssm_in_s = state_ssm.reshape(n_ssd, Bd, d_inner, N_ssd)

    def side_for(i):
        kind, j = i % 3, i // 3
        if kind == 0:
            return side_w(ssd_w_in, j, ssd_main, H_ssd)
        if kind == 1:
            return side_w(ml_w_in, j, ml_main, 2 * ML_HEADS)
        return None

    cos_p, sin_p = _rope_tables(n_meta + jnp.arange(Sp), ret_dk // 2)
    cos_s, sin_s = _rope_tables(PAST_LEN + jnp.arange(Ld), ret_dk // 2)
    cos_m, sin_m = _rope_tables(jnp.arange(n_meta), ret_dk // 2)

    ssm_p, conv_p, mC_p, mn_p, mm_p, ret_p = [], [], [], [], [], []
    conv_s, mC_s, mn_s, mm_s, ret_s = [], [], [], [], []
    ssm_s = None

    sw = side_for(0)
    h, side = _norm_call(x, g_pre=pre_norm_g[0], w_side=sw)
    for i in range(depth):
        kind, j = i % 3, i // 3
        if kind == 0:
            P = _matmul(h, ssd_w_in, j, ssd_main)
            args = (ssd_conv_w[j], ssd_conv_b[j], ssd_dt_bias[j], ssd_A_log[j], ssd_D[j], ssd_norm_g[j])
            zc = jnp.zeros((1, 1, SSD_CONV_W - 1, conv_dim), F32)
            zs = jnp.zeros((1, 1, d_inner, N_ssd), F32)
            ym, cx, cB, cC, sm = _ssd_call(rows_m, SSD_GB_PROMPT, P, side[n_p + n_s:], zc, zs, 0, *args, BF16)
            conv_m = jnp.concatenate([cx, cB, cC], axis=-1)
            yp, cx, cB, cC, sp = _ssd_call(rows_p, SSD_GB_PROMPT, P, side[:n_p], conv_m[None], sm[None], 0, *args, BF16)
            conv_p.append(jnp.concatenate([cx, cB, cC], axis=-1))
            ssm_p.append(sp.reshape(Bp, H_ssd, P_ssd, N_ssd))
            ys, cx, cB, cC, ssm_s = _ssd_call(rows_s, SSD_GB_SAMPLE, P, side[n_p:n_p + n_s],
                                              state_ssd_conv, ssm_in_s, j, *args, F32,
                                              ssm_alias=ssm_s, n_layers_out=n_ssd)
            conv_s.append(jnp.concatenate([cx, cB, cC], axis=-1))
            out = _matmul(join(yp, ys, ym), ssd_w_out, j, D)
        elif kind == 1:
            P = _matmul(h, ml_w_in, j, ml_main)
            args = (ml_b_gates[j], ml_norm_g[j])
            z = lambda *s: jnp.zeros(s, F32)
            ym, Cm, nm, mm = _ml_call(rows_m, ML_HB_PROMPT, P, side[n_p + n_s:], z(1, ML_HEADS, ml_dk, ml_dv),
                                      z(1, ML_HEADS, 1, ml_dk), z(1, ML_HEADS, 1, 1), *args, BF16)
            yp, Cp, np_, mp = _ml_call(rows_p, ML_HB_PROMPT, P, side[:n_p], Cm, nm, mm, *args, BF16)
            mC_p.append(Cp)
            mn_p.append(np_.reshape(Bp, ML_HEADS, ml_dk))
            mm_p.append(mp.reshape(Bp, ML_HEADS))
            ys, Cs, ns_, ms = _ml_call(rows_s, ML_HB_SAMPLE, P, side[n_p:n_p + n_s], state_mlstm_C[j],
                                       state_mlstm_n[j].reshape(Bd, ML_HEADS, 1, ml_dk),
                                       state_mlstm_m[j].reshape(Bd, ML_HEADS, 1, 1), *args, F32)
            mC_s.append(Cs)
            mn_s.append(ns_.reshape(Bd, ML_HEADS, ml_dk))
            mm_s.append(ms.reshape(Bd, ML_HEADS))
            out = _matmul(join(yp, ys, ym), ml_w_out, j, D)
        else:
            P = _matmul(h, ret_w_in, j, ret_main)
            ym, Sm = _ret_call(rows_m, RET_HB_PROMPT, P, jnp.zeros((1, RET_HEADS, ret_dk, ret_dv), F32), cos_m, sin_m,
                               ret_norm_g[j], BF16)
            yp, Sp_ = _ret_call(rows_p, RET_HB_PROMPT, P, Sm, cos_p, sin_p, ret_norm_g[j], BF16)
            ret_p.append(Sp_)
            ys, Ss = _ret_call(rows_s, RET_HB_SAMPLE, P, state_ret[j], cos_s, sin_s, ret_norm_g[j], F32)
            ret_s.append(Ss)
            out = _matmul(join(yp, ys, ym), ret_w_out, j, D)
        if i + 1 < depth:
            sw = side_for(i + 1)
            res = _norm_call(x, o=out, g_post=post_norm_g[i], g_pre=pre_norm_g[i + 1], w_side=sw)
            x, h = res[0], res[1]
            side = res[2] if sw is not None else None
        else:
            (y_prompt,) = _norm_call(x, o=out, g_post=post_norm_g[i], row_lo=0, n_rows=n_p)
            (y_sample,) = _norm_call(x, o=out, g_post=post_norm_g[i], row_lo=n_p, n_rows=n_s)

    y_prompt = y_prompt.reshape(Bp, Sp, D)
    y_sample = y_sample.reshape(Bd, Ld, D)
    return (y_prompt, y_sample, _stack(ssm_p), _stack(conv_p), _stack(mC_p), _stack(mn_p), _stack(mm_p), _stack(ret_p),
            ssm_s.reshape(n_ssd, Bd, H_ssd, P_ssd, N_ssd), _stack(conv_s), _stack(mC_s), _stack(mn_s), _stack(mm_s),
            _stack(ret_s))
```
